```python
import jax, jax.numpy as jnp
from jax import lax
import numpy as np

D_MODEL = 2048
BATCH = 4
SEQ = 8192
DEPTH = 1

D_MIX = D_MODEL
D_GMLP = D_MIX // 2
D_LRU = D_MIX - D_GMLP
CHUNK = 128
GMLP_HEAD_DIM = 128
N_GMLP_HEADS = D_GMLP // GMLP_HEAD_DIM
LRU_BLOCK = 128
N_LRU_BLOCKS = D_LRU // LRU_BLOCK
CONV_WIDTH = 4
LRU_C = 8.0
D_PLE = 256
EPS = 1e-6
D_IN_PROJ = 3 * D_GMLP + 2 * D_LRU

kernel_name = "hymba_gmlp_rglru_sandwich_ple"


def rmsnorm(x, g):
    xf = x.astype(jnp.float32)
    y = xf * lax.rsqrt(jnp.mean(xf * xf, axis=-1, keepdims=True) + EPS)
    return (y * g.astype(jnp.float32)).astype(x.dtype)


def layernorm(x, g, b):
    xf = x.astype(jnp.float32)
    mu = jnp.mean(xf, axis=-1, keepdims=True)
    xc = xf - mu
    y = xc * lax.rsqrt(jnp.mean(xc * xc, axis=-1, keepdims=True) + EPS)
    return (y * g.astype(jnp.float32) + b.astype(jnp.float32)).astype(x.dtype)


def gmlp_branch(u, v, ln_g, ln_b, w_s, b_s):
    bsz, s, _ = v.shape
    u = jax.nn.gelu(u)
    v = layernorm(jax.nn.gelu(v), ln_g, ln_b)
    vc = v.reshape(bsz, s // CHUNK, CHUNK, N_GMLP_HEADS, GMLP_HEAD_DIM)
    causal = jnp.tril(jnp.ones((CHUNK, CHUNK), dtype=bool))
    w = jnp.where(causal[None], w_s, jnp.zeros_like(w_s))
    mixed = jnp.einsum('hts,bcshd->bcthd', w, vc) + jnp.transpose(b_s)[None, None, :, :, None]
    return u * mixed.reshape(bsz, s, D_GMLP)


def _lin_rec_combine(left, right):
    a_l, b_l = left
    a_r, b_r = right
    return a_l * a_r, a_r * b_l + b_r


def rglru_branch(xb, conv_w, conv_b, w_a, b_a, w_x, b_x, lam):
    bsz, s, c = xb.shape
    xc = lax.conv_general_dilated(
        xb, conv_w, window_strides=(1,), padding=[(CONV_WIDTH - 1, 0)],
        dimension_numbers=('NWC', 'WIO', 'NWC'), feature_group_count=c) + conv_b
    xh = xc.reshape(bsz, s, N_LRU_BLOCKS, LRU_BLOCK)
    r = jax.nn.sigmoid(jnp.einsum('bshi,hij->bshj', xh, w_a) + b_a).reshape(bsz, s, c)
    i = jax.nn.sigmoid(jnp.einsum('bshi,hij->bshj', xh, w_x) + b_x).reshape(bsz, s, c)
    log_a = -LRU_C * r.astype(jnp.float32) * jax.nn.softplus(-lam.astype(jnp.float32))
    a = jnp.exp(log_a)
    mult = jnp.sqrt(-jnp.expm1(2.0 * log_a))
    is_first = (jnp.arange(s) == 0)[None, :, None]
    mult = jnp.where(is_first, jnp.ones_like(mult), mult)
    bt = mult * (i * xc).astype(jnp.float32)
    _, h = lax.associative_scan(_lin_rec_combine, (a, bt), axis=1)
    return h.astype(xb.dtype)


def setup_inputs(seed: int = 0) -> dict:
    key = jax.random.key(seed)
    ks = jax.random.split(key, 24)
    f32 = jnp.float32
    n = lambda k, shape, scale: jax.random.normal(k, shape, f32) * scale
    gain = lambda k, shape: 1.0 + 0.01 * jax.random.normal(k, shape, f32)
    x = jax.random.normal(ks[0], (BATCH, SEQ, D_MODEL), f32)
    p = jax.random.normal(ks[1], (DEPTH, BATCH, SEQ, D_PLE), f32)
    pre_g = gain(ks[2], (DEPTH, D_MODEL))
    w_in = n(ks[3], (DEPTH, D_MODEL, D_IN_PROJ), D_MODEL ** -0.5)
    gmlp_ln_g = gain(ks[4], (DEPTH, D_GMLP))
    gmlp_ln_b = n(ks[5], (DEPTH, D_GMLP), 0.01)
    gmlp_ws = n(ks[6], (DEPTH, N_GMLP_HEADS, CHUNK, CHUNK), CHUNK ** -0.5)
    gmlp_bs = gain(ks[7], (DEPTH, N_GMLP_HEADS, CHUNK))
    conv_w = n(ks[8], (DEPTH, CONV_WIDTH, 1, D_LRU), CONV_WIDTH ** -0.5)
    conv_b = n(ks[9], (DEPTH, D_LRU), 0.01)
    w_a = n(ks[10], (DEPTH, N_LRU_BLOCKS, LRU_BLOCK, LRU_BLOCK), LRU_BLOCK ** -0.5)
    b_a = n(ks[11], (DEPTH, N_LRU_BLOCKS, LRU_BLOCK), 0.01)
    w_x = n(ks[12], (DEPTH, N_LRU_BLOCKS, LRU_BLOCK, LRU_BLOCK), LRU_BLOCK ** -0.5)
    b_x = n(ks[13], (DEPTH, N_LRU_BLOCKS, LRU_BLOCK), 0.01)
    a0 = jax.random.uniform(ks[14], (DEPTH, D_LRU), f32, 0.9, 0.999)
    s0 = a0 ** (1.0 / LRU_C)
    lam = jnp.log(s0) - jnp.log1p(-s0)
    gmlp_out_g = gain(ks[15], (DEPTH, D_GMLP))
    lru_out_g = gain(ks[16], (DEPTH, D_LRU))
    w_out = n(ks[17], (DEPTH, D_MIX, D_MODEL), D_MIX ** -0.5)
    post_g = gain(ks[18], (DEPTH, D_MODEL))
    w_pe = n(ks[19], (DEPTH, D_PLE, D_MODEL), D_PLE ** -0.5)
    w_pg = n(ks[20], (DEPTH, D_MODEL, D_MODEL), D_MODEL ** -0.5)
    return {"x": x, "p": p, "pre_g": pre_g, "w_in": w_in, "gmlp_ln_g": gmlp_ln_g,
            "gmlp_ln_b": gmlp_ln_b, "gmlp_ws": gmlp_ws, "gmlp_bs": gmlp_bs,
            "conv_w": conv_w, "conv_b": conv_b, "w_a": w_a, "b_a": b_a, "w_x": w_x,
            "b_x": b_x, "lam": lam, "gmlp_out_g": gmlp_out_g, "lru_out_g": lru_out_g,
            "w_out": w_out, "post_g": post_g, "w_pe": w_pe, "w_pg": w_pg}


def reference(x, p, pre_g, w_in, gmlp_ln_g, gmlp_ln_b, gmlp_ws, gmlp_bs, conv_w, conv_b,
              w_a, b_a, w_x, b_x, lam, gmlp_out_g, lru_out_g, w_out, post_g, w_pe, w_pg):
    h = x
    splits = [D_GMLP, 2 * D_GMLP, 3 * D_GMLP, 3 * D_GMLP + D_LRU]
    for l in range(DEPTH):
        hn = rmsnorm(h, pre_g[l])
        z = hn @ w_in[l]
        u, v, gate_a, xb, gate_b = jnp.split(z, splits, axis=-1)
        ya = gmlp_branch(u, v, gmlp_ln_g[l], gmlp_ln_b[l], gmlp_ws[l], gmlp_bs[l]) * jax.nn.silu(gate_a)
        yb = rglru_branch(xb, conv_w[l], conv_b[l], w_a[l], b_a[l], w_x[l], b_x[l], lam[l]) * jax.nn.silu(gate_b)
        y = jnp.concatenate([rmsnorm(ya, gmlp_out_g[l]), rmsnorm(yb, lru_out_g[l])], axis=-1)
        h = h + rmsnorm(y @ w_out[l], post_g[l])
        h = h + (p[l] @ w_pe[l]) * jax.nn.sigmoid(h @ w_pg[l])
    return h
```

```python
import functools

import jax
import jax.numpy as jnp
from jax import lax
from jax.experimental import pallas as pl
from jax.experimental.pallas import tpu as pltpu

EPS = 1e-6
LRU_C = 8.0
CONV_WIDTH = 4
LANES = 128
SUBLANES = 8
HEAD = 128
VMEM_LIMIT = 60 * 1024 * 1024

SEQ_TILE = 512
OUT_TILE = 512
SEG = 68
SEG_ROWS = SUBLANES * SEG


def _rms(x, g):
    return x * lax.rsqrt(jnp.mean(x * x, axis=-1, keepdims=True) + EPS) * g


def _mixer_kernel(x_ref, pre_g_ref, w_in_ref, ln_g_ref, ln_b_ref, ws_ref, bias_ref,
                  cw_ref, cb_ref, wax_ref, ba_ref, bx_ref, lam_ref, ga_ref, gb_ref,
                  y_ref,
                  hn_s, zu_s, zv_s, zga_s, zgb_s, xb_s, wm_s, a_s, b_s, hl_s, al_s,
                  h_s, cm_s, hprev_s):
    T = SEQ_TILE
    D2 = zu_s.shape[1]
    NH = D2 // HEAD
    NCH = T // HEAD
    j = pl.program_id(1)
    first_step = jnp.logical_and(pl.program_id(0) == 0, j == 0)

    @pl.when(first_step)
    def _():
        row = lax.broadcasted_iota(jnp.int32, (HEAD, HEAD), 0)
        col = lax.broadcasted_iota(jnp.int32, (HEAD, HEAD), 1)
        for h in range(NH):
            wm_s[h] = jnp.where(row >= col, ws_ref[h], 0.0).astype(jnp.bfloat16)
            a_s[h, T:SEG_ROWS, :] = jnp.ones((SEG_ROWS - T, LANES), jnp.float32)
            b_s[h, T:SEG_ROWS, :] = jnp.zeros((SEG_ROWS - T, LANES), jnp.float32)

    @pl.when(j == 0)
    def _():
        xb_s[0:SUBLANES, :] = jnp.zeros((SUBLANES, D2), jnp.float32)
        hprev_s[...] = jnp.zeros_like(hprev_s)

    def norm_body(c, _):
        r = pl.ds(pl.multiple_of(c * HEAD, HEAD), HEAD)
        hn_s[r, :] = _rms(x_ref[r, :], pre_g_ref[...]).astype(jnp.bfloat16)
        return 0
    lax.fori_loop(0, NCH, norm_body, 0)

    hn = hn_s[...]
    zu_s[...] = jnp.dot(hn, w_in_ref[:, 0 * D2:1 * D2], preferred_element_type=jnp.float32)
    zv_s[...] = jnp.dot(hn, w_in_ref[:, 1 * D2:2 * D2], preferred_element_type=jnp.float32)
    zga_s[...] = jnp.dot(hn, w_in_ref[:, 2 * D2:3 * D2], preferred_element_type=jnp.float32)
    xb_s[SUBLANES:SUBLANES + T, :] = jnp.dot(hn, w_in_ref[:, 3 * D2:4 * D2],
                                              preferred_element_type=jnp.float32)
    zgb_s[...] = jnp.dot(hn, w_in_ref[:, 4 * D2:5 * D2], preferred_element_type=jnp.float32)

    def gmlp_body(c, _):
        r = pl.ds(pl.multiple_of(c * HEAD, HEAD), HEAD)
        u = jax.nn.gelu(zu_s[r, :])
        v = jax.nn.gelu(zv_s[r, :])
        mu = jnp.mean(v, axis=-1, keepdims=True)
        vc = v - mu
        vn = vc * lax.rsqrt(jnp.mean(vc * vc, axis=-1, keepdims=True) + EPS)
        vn = (vn * ln_g_ref[...] + ln_b_ref[...]).astype(jnp.bfloat16)
        mixed = jnp.concatenate(
            [jnp.dot(wm_s[h], vn[:, h * HEAD:(h + 1) * HEAD],
                     preferred_element_type=jnp.float32) for h in range(NH)], axis=-1)
        mixed = mixed + bias_ref[...]
        ya = u * mixed * jax.nn.silu(zga_s[r, :])
        y_ref[r, 0:D2] = _rms(ya, ga_ref[...]).astype(y_ref.dtype)
        return 0
    lax.fori_loop(0, NCH, gmlp_body, 0)

    sp = jax.nn.softplus(-lam_ref[...])
    neg_c_sp = -LRU_C * sp
    for c in range(NCH):
        base = c * HEAD + SUBLANES
        xc = cb_ref[...]
        for k in range(CONV_WIDTH):
            off = base - (CONV_WIDTH - 1) + k
            xc = xc + cw_ref[k:k + 1, :] * xb_s[off:off + HEAD, :]
        xcb = xc.astype(jnp.bfloat16)
        for h in range(NH):
            hs = slice(h * HEAD, (h + 1) * HEAD)
            g = jnp.dot(xcb[:, hs], wax_ref[h], preferred_element_type=jnp.float32)
            rg = jax.nn.sigmoid(g[:, :HEAD] + ba_ref[:, hs])
            ig = jax.nn.sigmoid(g[:, HEAD:] + bx_ref[:, hs])
            log_a = rg * neg_c_sp[:, hs]
            a = jnp.exp(log_a)
            th = jnp.tanh(-log_a)
            mult = jnp.sqrt(2.0 * th / (1.0 + th))
            if c == 0:
                rowi = lax.broadcasted_iota(jnp.int32, (HEAD, HEAD), 0)
                mult = jnp.where(jnp.logical_and(rowi == 0, j == 0), 1.0, mult)
            a_s[h, c * HEAD:(c + 1) * HEAD, :] = a
            b_s[h, c * HEAD:(c + 1) * HEAD, :] = mult * (ig * xc[:, hs])
    xb_s[0:SUBLANES, :] = xb_s[T:T + SUBLANES, :]

    def scan1(i, carry):
        hs_, as_ = carry
        nh, na = [], []
        for h in range(NH):
            a = a_s[h, pl.ds(i, SUBLANES, stride=SEG), :]
            b = b_s[h, pl.ds(i, SUBLANES, stride=SEG), :]
            hh = a * hs_[h] + b
            aa = as_[h] * a
            o = pl.ds(pl.multiple_of(i * SUBLANES, SUBLANES), SUBLANES)
            hl_s[h, o, :] = hh
            al_s[h, o, :] = aa
            nh.append(hh)
            na.append(aa)
        return tuple(nh), tuple(na)
    zeros = tuple(jnp.zeros((SUBLANES, LANES), jnp.float32) for _ in range(NH))
    ones = tuple(jnp.ones((SUBLANES, LANES), jnp.float32) for _ in range(NH))
    hfin, afin = lax.fori_loop(0, SEG, scan1, (zeros, ones))

    for h in range(NH):
        cst = hprev_s[h, 0:1, :]
        for s in range(SUBLANES):
            cm_s[h, s:s + 1, :] = cst
            cst = afin[h][s:s + 1, :] * cst + hfin[h][s:s + 1, :]
        hprev_s[h, 0:1, :] = cst

    def scan2(i, _):
        for h in range(NH):
            o = pl.ds(pl.multiple_of(i * SUBLANES, SUBLANES), SUBLANES)
            h_s[h, pl.ds(i, SUBLANES, stride=SEG), :] = hl_s[h, o, :] + al_s[h, o, :] * cm_s[h]
        return 0
    lax.fori_loop(0, SEG, scan2, 0)

    def lru_out_body(c, _):
        r = pl.ds(pl.multiple_of(c * HEAD, HEAD), HEAD)
        hh = jnp.concatenate([h_s[h, r, :] for h in range(NH)], axis=-1)
        yb = hh * jax.nn.silu(zgb_s[r, :])
        y_ref[r, D2:2 * D2] = _rms(yb, gb_ref[...]).astype(y_ref.dtype)
        return 0
    lax.fori_loop(0, NCH, lru_out_body, 0)


def _output_kernel(y_ref, x_ref, p_ref, w_out_ref, post_g_ref, w_pe_ref, w_pg_ref, o_ref):
    o = jnp.dot(y_ref[...], w_out_ref[...], preferred_element_type=jnp.float32)
    h1 = x_ref[...] + _rms(o, post_g_ref[...])
    g = jnp.dot(h1.astype(jnp.bfloat16), w_pg_ref[...], preferred_element_type=jnp.float32)
    pe = jnp.dot(p_ref[...].astype(jnp.bfloat16), w_pe_ref[...],
                 preferred_element_type=jnp.float32)
    o_ref[...] = h1 + pe * jax.nn.sigmoid(g)


def _const_spec(shape, ngrid):
    zeros = (0,) * len(shape)
    if ngrid == 2:
        return pl.BlockSpec(shape, lambda b, j: zeros, pipeline_mode=pl.Buffered(1))
    return pl.BlockSpec(shape, lambda i: zeros, pipeline_mode=pl.Buffered(1))


def _mixer(x, pre_g, w_in, ln_g, ln_b, ws, bias_full, cw, cb, wax, b_a, b_x, lam, ga, gb):
    B, S, D = x.shape
    D2 = ln_g.shape[-1]
    NH = D2 // HEAD
    T = SEQ_TILE
    f32 = jnp.float32
    consts = [pre_g, w_in, ln_g, ln_b, ws, bias_full, cw, cb, wax, b_a, b_x, lam, ga, gb]
    return pl.pallas_call(
        _mixer_kernel,
        grid=(B, S // T),
        in_specs=[pl.BlockSpec((None, T, D), lambda b, j: (b, j, 0))]
        + [_const_spec(c.shape, 2) for c in consts],
        out_specs=pl.BlockSpec((None, T, 2 * D2), lambda b, j: (b, j, 0)),
        out_shape=jax.ShapeDtypeStruct((B, S, 2 * D2), jnp.bfloat16),
        scratch_shapes=[
            pltpu.VMEM((T, D), jnp.bfloat16),
            pltpu.VMEM((T, D2), f32),
            pltpu.VMEM((T, D2), f32),
            pltpu.VMEM((T, D2), f32),
            pltpu.VMEM((T, D2), f32),
            pltpu.VMEM((T + SUBLANES, D2), f32),
            pltpu.VMEM((NH, HEAD, HEAD), jnp.bfloat16),
            pltpu.VMEM((NH, SEG_ROWS, LANES), f32),
            pltpu.VMEM((NH, SEG_ROWS, LANES), f32),
            pltpu.VMEM((NH, SEG_ROWS, LANES), f32),
            pltpu.VMEM((NH, SEG_ROWS, LANES), f32),
            pltpu.VMEM((NH, SEG_ROWS, LANES), f32),
            pltpu.VMEM((NH, SUBLANES, LANES), f32),
            pltpu.VMEM((NH, SUBLANES, LANES), f32),
        ],
        compiler_params=pltpu.CompilerParams(
            dimension_semantics=("arbitrary", "arbitrary"),
            vmem_limit_bytes=VMEM_LIMIT),
        name="mixer",
    )(x, *consts)


def _output(y, x, p, w_out, post_g, w_pe, w_pg):
    N, D = x.shape
    DP = p.shape[-1]
    T = OUT_TILE
    consts = [w_out, post_g, w_pe, w_pg]
    return pl.pallas_call(
        _output_kernel,
        grid=(N // T,),
        in_specs=[pl.BlockSpec((T, y.shape[-1]), lambda i: (i, 0)),
                  pl.BlockSpec((T, D), lambda i: (i, 0)),
                  pl.BlockSpec((T, DP), lambda i: (i, 0))]
        + [_const_spec(c.shape, 1) for c in consts],
        out_specs=pl.BlockSpec((T, D), lambda i: (i, 0)),
        out_shape=jax.ShapeDtypeStruct((N, D), jnp.float32),
        compiler_params=pltpu.CompilerParams(
            dimension_semantics=("parallel",),
            vmem_limit_bytes=VMEM_LIMIT),
        name="output",
    )(y, x, p, *consts)


def kernel(x, p, pre_g, w_in, gmlp_ln_g, gmlp_ln_b, gmlp_ws, gmlp_bs, conv_w, conv_b, w_a, b_a, w_x, b_x, lam, gmlp_out_g, lru_out_g, w_out, post_g, w_pe, w_pg):
    B, S, D = x.shape
    depth = w_in.shape[0]
    D2 = gmlp_ln_g.shape[-1]
    bf16 = jnp.bfloat16
    row = lambda a: a.reshape(1, -1)
    h = x
    for l in range(depth):
        bias_full = jnp.repeat(jnp.transpose(gmlp_bs[l]), HEAD, axis=1)
        wax = jnp.concatenate([w_a[l], w_x[l]], axis=-1).astype(bf16)
        y = _mixer(h, row(pre_g[l]), w_in[l].astype(bf16), row(gmlp_ln_g[l]), row(gmlp_ln_b[l]),
                   gmlp_ws[l], bias_full, conv_w[l].reshape(CONV_WIDTH, D2), row(conv_b[l]),
                   wax, row(b_a[l]), row(b_x[l]), row(lam[l]),
                   row(gmlp_out_g[l]), row(lru_out_g[l]))
        h = _output(y.reshape(B * S, 2 * D2), h.reshape(B * S, D), p[l].reshape(B * S, -1),
                    w_out[l].astype(bf16), row(post_g[l]), w_pe[l].astype(bf16),
                    w_pg[l].astype(bf16)).reshape(B, S, D)
    return h
```

```python
import collections

import jax
import jax.numpy as jnp
from jax import lax
from jax.experimental import pallas as pl
from jax.experimental.pallas import tpu as pltpu

EPS = 1e-6
LRU_C = 8.0
CONV_WIDTH = 4
LANES = 128
SUBLANES = 8
HEAD = 128
VMEM_LIMIT = 60 * 1024 * 1024

TILE = 256
OUT_TILE = 2 * TILE
SEG = 36
SEG_ROWS = SUBLANES * SEG
NCH = TILE // HEAD
DOT_N = 512
DOT_COST = 1024
PIECE_COST = dict(gmlp=1500, gate=850, ends=350, scan=450, lru_out=470, norm=400)

ZBuf = collections.namedtuple("ZBuf", "u v ga gb xb hdr")


def _rms(x, g):
    return x * lax.rsqrt(jnp.mean(x * x, axis=-1, keepdims=True) + EPS) * g


def _mixer_kernel(x_ref, pre_g_ref, w_in_ref, ln_g_ref, ln_b_ref, ws_ref, bias_ref,
                  cw_ref, cb_ref, wax_ref, ba_ref, bx_ref, lam_ref, ga_ref, gb_ref,
                  y_ref, *scratch, tiles_per_seq):
    z = (ZBuf(*scratch[0:6]), ZBuf(*scratch[6:12]))
    hn = scratch[12:14]
    wm_s, ncs_s, a_s, b_s, h_s, hprev_s = scratch[14:]
    T = TILE
    D2 = z[0].u.shape[-1]
    NH = D2 // HEAD
    g = pl.program_id(0)

    @pl.when(g == 0)
    def _():
        row = lax.broadcasted_iota(jnp.int32, (HEAD, HEAD), 0)
        col = lax.broadcasted_iota(jnp.int32, (HEAD, HEAD), 1)
        for h in range(NH):
            wm_s[h] = jnp.where(row >= col, ws_ref[h], 0.0).astype(jnp.bfloat16)
            a_s[h, T:SEG_ROWS, :] = jnp.ones((SEG_ROWS - T, LANES), jnp.float32)
            b_s[h, T:SEG_ROWS, :] = jnp.zeros((SEG_ROWS - T, LANES), jnp.float32)
        ncs_s[...] = -LRU_C * jax.nn.softplus(-lam_ref[...])
        hn[1][...] = jnp.zeros_like(hn[1])
        for ref in z[0]:
            ref[...] = jnp.zeros_like(ref)
        hprev_s[...] = jnp.zeros_like(hprev_s)

    def norm_piece(row0, c, hn_s):
        def run():
            hn_s[c * HEAD:(c + 1) * HEAD, :] = _rms(
                x_ref[row0 + c * HEAD:row0 + (c + 1) * HEAD, :],
                pre_g_ref[...]).astype(jnp.bfloat16)
        return run

    def dot_piece(hn_s, dst, col0):
        def run():
            dst[:, col0 % D2:col0 % D2 + DOT_N] = jnp.dot(
                hn_s[...], w_in_ref[:, col0:col0 + DOT_N], preferred_element_type=jnp.float32)
        return run

    def gmlp_piece(zb, out0, c):
        def run():
            r = slice(c * HEAD, (c + 1) * HEAD)
            u = jax.nn.gelu(zb.u[r, :])
            v = jax.nn.gelu(zb.v[r, :])
            mu = jnp.mean(v, axis=-1, keepdims=True)
            vc = v - mu
            vn = vc * lax.rsqrt(jnp.mean(vc * vc, axis=-1, keepdims=True) + EPS)
            vn = (vn * ln_g_ref[...] + ln_b_ref[...]).astype(jnp.bfloat16)
            mixed = jnp.concatenate(
                [jnp.dot(wm_s[h], vn[:, h * HEAD:(h + 1) * HEAD],
                         preferred_element_type=jnp.float32) for h in range(NH)], axis=-1)
            mixed = mixed + bias_ref[...]
            ya = u * mixed * jax.nn.silu(zb.ga[r, :])
            y_ref[out0 + c * HEAD:out0 + (c + 1) * HEAD, 0:D2] = _rms(
                ya, ga_ref[...]).astype(y_ref.dtype)
        return run

    def gate_piece(zb, c, half, first):
        def run():
            ls = slice(half * (D2 // 2), (half + 1) * (D2 // 2))
            if c == 0:
                hdr = zb.hdr[:, ls]
                if first is not None:
                    hdr = jnp.where(first, 0.0, hdr)
                win = jnp.concatenate([hdr, zb.xb[0:HEAD, ls]], axis=0)
            else:
                win = zb.xb[c * HEAD - SUBLANES:(c + 1) * HEAD, ls]
            xc = cb_ref[:, ls]
            for k in range(CONV_WIDTH):
                off = SUBLANES - (CONV_WIDTH - 1) + k
                xc = xc + cw_ref[k:k + 1, ls] * win[off:off + HEAD, :]
            xcb = xc.astype(jnp.bfloat16)
            for hh in range(NH // 2):
                h = half * (NH // 2) + hh
                hs = slice(h * HEAD, (h + 1) * HEAD)
                hl = slice(hh * HEAD, (hh + 1) * HEAD)
                gt = jnp.dot(xcb[:, hl], wax_ref[h], preferred_element_type=jnp.float32)
                rg = jax.nn.sigmoid(gt[:, :HEAD] + ba_ref[:, hs])
                ig = jax.nn.sigmoid(gt[:, HEAD:] + bx_ref[:, hs])
                log_a = rg * ncs_s[:, hs]
                a = jnp.exp(log_a)
                th = jnp.tanh(-log_a)
                mult = jnp.sqrt(2.0 * th / (1.0 + th))
                if c == 0 and first is not None:
                    rowi = lax.broadcasted_iota(jnp.int32, (HEAD, HEAD), 0)
                    mult = jnp.where(jnp.logical_and(rowi == 0, first), 1.0, mult)
                a_s[h, c * HEAD:(c + 1) * HEAD, :] = a
                b_s[h, c * HEAD:(c + 1) * HEAD, :] = mult * (ig * xc[:, hl])
        return run

    def scan_pieces(zb, hdr_next, first):
        st = {}

        def ends():
            hdr_next[...] = zb.xb[T - SUBLANES:T, :]
            hfin = [jnp.zeros((SUBLANES, LANES), jnp.float32) for _ in range(NH)]
            afin = [jnp.ones((SUBLANES, LANES), jnp.float32) for _ in range(NH)]
            for i in range(SEG):
                for h in range(NH):
                    a = a_s[h, pl.ds(i, SUBLANES, stride=SEG), :]
                    b = b_s[h, pl.ds(i, SUBLANES, stride=SEG), :]
                    hfin[h] = a * hfin[h] + b
                    afin[h] = afin[h] * a
            st["hfin"], st["afin"] = hfin, afin

        def scan():
            hcur = []
            for h in range(NH):
                cst = hprev_s[h, 0:1, :]
                if first is not None:
                    cst = jnp.where(first, 0.0, cst)
                rows = []
                for s in range(SUBLANES):
                    rows.append(cst)
                    cst = st["afin"][h][s:s + 1, :] * cst + st["hfin"][h][s:s + 1, :]
                hprev_s[h, 0:1, :] = cst
                hcur.append(jnp.concatenate(rows, axis=0))
            for i in range(SEG):
                for h in range(NH):
                    a = a_s[h, pl.ds(i, SUBLANES, stride=SEG), :]
                    b = b_s[h, pl.ds(i, SUBLANES, stride=SEG), :]
                    hcur[h] = a * hcur[h] + b
                    h_s[h, pl.ds(i, SUBLANES, stride=SEG), :] = hcur[h]
        return ends, scan

    def lru_out_piece(zb, out0, c):
        def run():
            r = slice(c * HEAD, (c + 1) * HEAD)
            hh = jnp.concatenate([h_s[h, r, :] for h in range(NH)], axis=-1)
            yb = hh * jax.nn.silu(zb.gb[r, :])
            y_ref[out0 + c * HEAD:out0 + (c + 1) * HEAD, D2:2 * D2] = _rms(
                yb, gb_ref[...]).astype(y_ref.dtype)
        return run

    def half_step(par, first):
        zb_out, zb_in = z[1 - par], z[par]
        dsts = [zb_out.u, zb_out.v, zb_out.ga, zb_out.xb, zb_out.gb]
        mxu = [(DOT_COST, dot_piece(hn[1 - par], dsts[col0 // D2], col0))
               for col0 in range(0, 5 * D2, DOT_N)]
        ends, scan = scan_pieces(zb_in, zb_out.hdr, first)
        cost = PIECE_COST
        vpu = ([(cost["gmlp"], gmlp_piece(zb_in, par * T, c)) for c in range(NCH)]
               + [(cost["gate"], gate_piece(zb_in, c, half, first))
                  for c in range(NCH) for half in range(2)]
               + [(cost["ends"], ends), (cost["scan"], scan)]
               + [(cost["lru_out"], lru_out_piece(zb_in, par * T, c)) for c in range(NCH)]
               + [(cost["norm"], norm_piece(par * T, c, hn[par])) for c in range(NCH)])
        t_mxu = t_vpu = 0
        while mxu or vpu:
            if mxu and (t_mxu <= t_vpu or not vpu):
                t, run = mxu.pop(0)
                t_mxu += t
            else:
                t, run = vpu.pop(0)
                t_vpu += t
            run()

    half_step(0, (2 * g - 2) % tiles_per_seq == 0)
    half_step(1, None)


def _output_kernel(y_ref, x_ref, p_ref, w_out_ref, post_g_ref, w_pe_ref, w_pg_ref, o_ref):
    o = jnp.dot(y_ref[...], w_out_ref[...], preferred_element_type=jnp.float32)
    h1 = x_ref[...] + _rms(o, post_g_ref[...])
    g = jnp.dot(h1.astype(jnp.bfloat16), w_pg_ref[...], preferred_element_type=jnp.float32)
    pe = jnp.dot(p_ref[...].astype(jnp.bfloat16), w_pe_ref[...],
                 preferred_element_type=jnp.float32)
    o_ref[...] = h1 + pe * jax.nn.sigmoid(g)


def _const_spec(shape):
    zeros = (0,) * len(shape)
    return pl.BlockSpec(shape, lambda i: zeros, pipeline_mode=pl.Buffered(1))


def _mixer(x, seq_len, pre_g, w_in, ln_g, ln_b, ws, bias_full, cw, cb, wax, b_a, b_x, lam, ga, gb):
    N, D = x.shape
    D2 = ln_g.shape[-1]
    NH = D2 // HEAD
    T = TILE
    assert N % (2 * T) == 0 and seq_len % (2 * T) == 0
    G = N // (2 * T)
    f32 = jnp.float32
    consts = [pre_g, w_in, ln_g, ln_b, ws, bias_full, cw, cb, wax, b_a, b_x, lam, ga, gb]
    kern = lambda *refs: _mixer_kernel(*refs, tiles_per_seq=seq_len // T)
    zbuf = [pltpu.VMEM((T, D2), f32)] * 5 + [pltpu.VMEM((SUBLANES, D2), f32)]
    return pl.pallas_call(
        kern,
        grid=(G + 1,),
        in_specs=[pl.BlockSpec((2 * T, D), lambda g: (jnp.minimum(g, G - 1), 0))]
        + [_const_spec(c.shape) for c in consts],
        out_specs=pl.BlockSpec((2 * T, 2 * D2), lambda g: (jnp.maximum(g - 1, 0), 0)),
        out_shape=jax.ShapeDtypeStruct((N, 2 * D2), jnp.bfloat16),
        scratch_shapes=zbuf + zbuf + [
            pltpu.VMEM((T, D), jnp.bfloat16),
            pltpu.VMEM((T, D), jnp.bfloat16),
            pltpu.VMEM((NH, HEAD, HEAD), jnp.bfloat16),
            pltpu.VMEM((1, D2), f32),
            pltpu.VMEM((NH, SEG_ROWS, LANES), f32),
            pltpu.VMEM((NH, SEG_ROWS, LANES), f32),
            pltpu.VMEM((NH, SEG_ROWS, LANES), f32),
            pltpu.VMEM((NH, SUBLANES, LANES), f32),
        ],
        compiler_params=pltpu.CompilerParams(
            dimension_semantics=("arbitrary",),
            vmem_limit_bytes=VMEM_LIMIT),
        name="mixer",
    )(x, *consts)


def _output(y, x, p, w_out, post_g, w_pe, w_pg):
    N, D = x.shape
    DP = p.shape[-1]
    T = OUT_TILE
    consts = [w_out, post_g, w_pe, w_pg]
    return pl.pallas_call(
        _output_kernel,
        grid=(N // T,),
        in_specs=[pl.BlockSpec((T, y.shape[-1]), lambda i: (i, 0)),
                  pl.BlockSpec((T, D), lambda i: (i, 0)),
                  pl.BlockSpec((T, DP), lambda i: (i, 0))]
        + [_const_spec(c.shape) for c in consts],
        out_specs=pl.BlockSpec((T, D), lambda i: (i, 0)),
        out_shape=jax.ShapeDtypeStruct((N, D), jnp.float32),
        compiler_params=pltpu.CompilerParams(
            dimension_semantics=("parallel",),
            vmem_limit_bytes=VMEM_LIMIT),
        name="output",
    )(y, x, p, *consts)


def kernel(x, p, pre_g, w_in, gmlp_ln_g, gmlp_ln_b, gmlp_ws, gmlp_bs, conv_w, conv_b, w_a, b_a, w_x, b_x, lam, gmlp_out_g, lru_out_g, w_out, post_g, w_pe, w_pg):
    B, S, D = x.shape
    depth = w_in.shape[0]
    D2 = gmlp_ln_g.shape[-1]
    bf16 = jnp.bfloat16
    row = lambda a: a.reshape(1, -1)
    h = x.reshape(B * S, D)
    for l in range(depth):
        bias_full = jnp.repeat(jnp.transpose(gmlp_bs[l]), HEAD, axis=1)
        wax = jnp.concatenate([w_a[l], w_x[l]], axis=-1).astype(bf16)
        y = _mixer(
            h, S, row(pre_g[l]), w_in[l].astype(bf16), row(gmlp_ln_g[l]), row(gmlp_ln_b[l]),
            gmlp_ws[l], bias_full, conv_w[l].reshape(CONV_WIDTH, D2), row(conv_b[l]),
            wax, row(b_a[l]), row(b_x[l]), row(lam[l]),
            row(gmlp_out_g[l]), row(lru_out_g[l]))
        h = _output(y, h, p[l].reshape(B * S, -1),
                    w_out[l].astype(bf16), row(post_g[l]), w_pe[l].astype(bf16),
                    w_pg[l].astype(bf16))
    return h.reshape(B, S, D)
```

```python
import collections
import math

import jax
import jax.numpy as jnp
from jax import lax
from jax.experimental import pallas as pl
from jax.experimental.pallas import tpu as pltpu

EPS = 1e-6
LRU_C = 8.0
CONV_WIDTH = 4
LANES = 128
SUBLANES = 8
HEAD = 128
VMEM_LIMIT = 60 * 1024 * 1024

TILE = 256
OUT_TILE = 2 * TILE
SEG = 34
SEG_ROWS = SUBLANES * SEG
XB_ROWS = SUBLANES + SEG_ROWS
NCH = TILE // HEAD
DOT_N = 512
DOT_COST = 1024
PIECE_COST = dict(gmlp_v=600, gmlp_mix=300, gmlp_u=750, gate=700, ends=300, scan=400,
                  lru_out=470, norm=400)

LOG2E = math.log2(math.e)
GELU_K1 = -2.0 * math.sqrt(2.0 / math.pi) * LOG2E
GELU_K2 = GELU_K1 * 0.044715

ZBuf = collections.namedtuple("ZBuf", "u v ga gb xb")


def _rms(x, g):
    return x * lax.rsqrt(jnp.mean(x * x, axis=-1, keepdims=True) + EPS) * g


def _gelu(x):
    return x / (1.0 + jnp.exp2(x * (GELU_K1 + GELU_K2 * (x * x))))


def _mixer_kernel(x_ref, pre_g_ref, w_in_ref, ln_g_ref, ln_b_ref, ws_ref, bias_ref,
                  cw_ref, cb_ref, wax_ref, ba_ref, bx_ref, lam_ref, ga_ref, gb_ref,
                  y_ref, *scratch, tiles_per_seq):
    z = (ZBuf(*scratch[0:5]), ZBuf(*scratch[5:10]))
    hn = scratch[10:12]
    wm_s, dec_s, cwb_s, vn_s, mix_s, xcp_s, a_s, b_s, h_s, hprev_s = scratch[12:]
    T = TILE
    D2 = z[0].u.shape[-1]
    NH = D2 // HEAD
    g = pl.program_id(0)

    @pl.when(g == 0)
    def _():
        row = lax.broadcasted_iota(jnp.int32, (HEAD, HEAD), 0)
        col = lax.broadcasted_iota(jnp.int32, (HEAD, HEAD), 1)
        for h in range(NH):
            hs = slice(h * HEAD, (h + 1) * HEAD)
            wm_s[h] = jnp.where(row >= col, ws_ref[h], 0.0).astype(jnp.bfloat16)
            for k in range(CONV_WIDTH):
                cwb_s[k, h] = jnp.broadcast_to(cw_ref[k:k + 1, hs], (SUBLANES, LANES))
            cwb_s[CONV_WIDTH, h] = jnp.broadcast_to(cb_ref[:, hs], (SUBLANES, LANES))
        dec = LRU_C * jax.nn.softplus(-lam_ref[...])
        dec_s[0:1, :] = dec
        dec_s[1:2, :] = dec * (-LOG2E)
        hn[1][...] = jnp.zeros_like(hn[1])
        for ref in z[0]:
            ref[...] = jnp.zeros_like(ref)
        z[1].xb[...] = jnp.zeros_like(z[1].xb)
        hprev_s[...] = jnp.zeros_like(hprev_s)

    def norm_piece(row0, c, hn_s):
        def run():
            hn_s[c * HEAD:(c + 1) * HEAD, :] = _rms(
                x_ref[row0 + c * HEAD:row0 + (c + 1) * HEAD, :],
                pre_g_ref[...]).astype(jnp.bfloat16)
        return run

    def dot_piece(hn_s, zb, col0):
        def run():
            res = jnp.dot(hn_s[...], w_in_ref[:, col0:col0 + DOT_N],
                          preferred_element_type=jnp.float32)
            grp, off = divmod(col0, D2)
            if grp == 3:
                for hh in range(DOT_N // HEAD):
                    zb.xb[off // HEAD + hh, SUBLANES:SUBLANES + T, :] = (
                        res[:, hh * HEAD:(hh + 1) * HEAD])
            else:
                dst = (zb.u, zb.v, zb.ga, None, zb.gb)[grp]
                dst[:, off:off + DOT_N] = res
        return run

    def gmlp_v_piece(zb, c):
        def run():
            r = slice(c * HEAD, (c + 1) * HEAD)
            v = _gelu(zb.v[r, :])
            mu = jnp.mean(v, axis=-1, keepdims=True)
            vc = v - mu
            vn = vc * lax.rsqrt(jnp.mean(vc * vc, axis=-1, keepdims=True) + EPS)
            vn_s[r, :] = (vn * ln_g_ref[...] + ln_b_ref[...]).astype(jnp.bfloat16)
        return run

    def gmlp_mix_piece():
        def run():
            for h in range(NH):
                hs = slice(h * HEAD, (h + 1) * HEAD)
                rhs = jnp.concatenate(
                    [vn_s[c * HEAD:(c + 1) * HEAD, hs] for c in range(NCH)], axis=-1)
                m = jnp.dot(wm_s[h], rhs, preferred_element_type=jnp.float32)
                for c in range(NCH):
                    mix_s[c * HEAD:(c + 1) * HEAD, hs] = m[:, c * HEAD:(c + 1) * HEAD]
        return run

    def gmlp_u_piece(zb, out0, c):
        def run():
            r = slice(c * HEAD, (c + 1) * HEAD)
            mixed = mix_s[r, :] + bias_ref[...]
            ya = _gelu(zb.u[r, :]) * mixed * jax.nn.silu(zb.ga[r, :])
            y_ref[out0 + c * HEAD:out0 + (c + 1) * HEAD, 0:D2] = _rms(
                ya, ga_ref[...]).astype(y_ref.dtype)
        return run

    def gate_piece(zb, zb_next, heads, first):
        def run():
            rowi = lax.broadcasted_iota(jnp.int32, (SUBLANES, LANES), 0)
            for h in heads:
                hs = slice(h * HEAD, (h + 1) * HEAD)
                if first is not None:
                    zb.xb[h, 0:SUBLANES, :] = jnp.where(first, 0.0, zb.xb[h, 0:SUBLANES, :])
                zb_next.xb[h, 0:SUBLANES, :] = zb.xb[h, T:T + SUBLANES, :]
                taps = [zb.xb[h, pl.ds(SUBLANES - (CONV_WIDTH - 1) + j, SUBLANES, stride=SEG), :]
                        for j in range(CONV_WIDTH - 1)]
                for i in range(SEG):
                    taps.append(zb.xb[h, pl.ds(SUBLANES + i, SUBLANES, stride=SEG), :])
                    xc = cwb_s[CONV_WIDTH, h]
                    for k in range(CONV_WIDTH):
                        xc = xc + cwb_s[k, h] * taps[i + k]
                    xcp_s[h, i * SUBLANES:(i + 1) * SUBLANES, :] = xc
                xc = xcp_s[h]
                gt = jnp.dot(xc.astype(jnp.bfloat16), wax_ref[h],
                             preferred_element_type=jnp.float32)
                rg = jax.nn.sigmoid(gt[:, :HEAD] + ba_ref[:, hs])
                ig = jax.nn.sigmoid(gt[:, HEAD:] + bx_ref[:, hs])
                a = jnp.exp2(rg * dec_s[1:2, hs])
                th = jnp.tanh(rg * dec_s[0:1, hs])
                mult = lax.rsqrt(0.5 + 0.5 / jnp.maximum(th, 1e-30))
                gx = ig * xc
                b = mult * gx
                pad0 = (T - (SUBLANES - 1) * SEG) * SUBLANES
                pad = rowi == SUBLANES - 1
                a_s[h, 0:pad0, :] = a[0:pad0]
                b_s[h, SUBLANES:pad0, :] = b[SUBLANES:pad0]
                for i in range(pad0 // SUBLANES, SEG):
                    r = slice(i * SUBLANES, (i + 1) * SUBLANES)
                    a_s[h, r, :] = jnp.where(pad, 1.0, a[r])
                    b_s[h, r, :] = jnp.where(pad, 0.0, b[r])
                b0 = b[0:SUBLANES]
                if first is not None:
                    b0 = jnp.where(jnp.logical_and(rowi == 0, first), gx[0:SUBLANES], b0)
                b_s[h, 0:SUBLANES, :] = b0
        return run

    def scan_pieces(first):
        st = {}

        def ends():
            hfin = [jnp.zeros((SUBLANES, LANES), jnp.float32) for _ in range(NH)]
            afin = [jnp.ones((SUBLANES, LANES), jnp.float32) for _ in range(NH)]
            for i in range(SEG):
                r = slice(i * SUBLANES, (i + 1) * SUBLANES)
                for h in range(NH):
                    a = a_s[h, r, :]
                    hfin[h] = a * hfin[h] + b_s[h, r, :]
                    afin[h] = afin[h] * a
            st["hfin"], st["afin"] = hfin, afin

        def scan():
            hcur = []
            for h in range(NH):
                cst = hprev_s[h, 0:1, :]
                if first is not None:
                    cst = jnp.where(first, 0.0, cst)
                rows = []
                for s in range(SUBLANES):
                    rows.append(cst)
                    cst = st["afin"][h][s:s + 1, :] * cst + st["hfin"][h][s:s + 1, :]
                hprev_s[h, 0:1, :] = cst
                hcur.append(jnp.concatenate(rows, axis=0))
            for i in range(SEG):
                r = slice(i * SUBLANES, (i + 1) * SUBLANES)
                for h in range(NH):
                    hcur[h] = a_s[h, r, :] * hcur[h] + b_s[h, r, :]
                    h_s[h, pl.ds(i, SUBLANES, stride=SEG), :] = hcur[h]
        return ends, scan

    def lru_out_piece(zb, out0, c):
        def run():
            r = slice(c * HEAD, (c + 1) * HEAD)
            hh = jnp.concatenate([h_s[h, r, :] for h in range(NH)], axis=-1)
            yb = hh * jax.nn.silu(zb.gb[r, :])
            y_ref[out0 + c * HEAD:out0 + (c + 1) * HEAD, D2:2 * D2] = _rms(
                yb, gb_ref[...]).astype(y_ref.dtype)
        return run

    def half_step(par, first):
        zb_out, zb_in = z[1 - par], z[par]
        mxu = [(DOT_COST, dot_piece(hn[1 - par], zb_out, col0))
               for col0 in range(0, 5 * D2, DOT_N)]
        ends, scan = scan_pieces(first)
        cost = PIECE_COST
        vpu = ([(cost["gmlp_v"], gmlp_v_piece(zb_in, c)) for c in range(NCH)]
               + [(cost["gmlp_mix"], gmlp_mix_piece())]
               + [(cost["gmlp_u"], gmlp_u_piece(zb_in, par * T, c)) for c in range(NCH)]
               + [(cost["gate"], gate_piece(zb_in, zb_out, (h, h + 1), first))
                  for h in range(0, NH, 2)]
               + [(cost["ends"], ends), (cost["scan"], scan)]
               + [(cost["lru_out"], lru_out_piece(zb_in, par * T, c)) for c in range(NCH)]
               + [(cost["norm"], norm_piece(par * T, c, hn[par])) for c in range(NCH)])
        t_mxu = t_vpu = 0
        while mxu or vpu:
            if mxu and (t_mxu <= t_vpu or not vpu):
                t, run = mxu.pop(0)
                t_mxu += t
            else:
                t, run = vpu.pop(0)
                t_vpu += t
            run()

    half_step(0, (2 * g - 2) % tiles_per_seq == 0)
    half_step(1, None)


def _output_kernel(y_ref, x_ref, p_ref, w_out_ref, post_g_ref, w_pe_ref, w_pg_ref, o_ref):
    o = jnp.dot(y_ref[...], w_out_ref[...], preferred_element_type=jnp.float32)
    h1 = x_ref[...] + _rms(o, post_g_ref[...])
    g = jnp.dot(h1.astype(jnp.bfloat16), w_pg_ref[...], preferred_element_type=jnp.float32)
    pe = jnp.dot(p_ref[...].astype(jnp.bfloat16), w_pe_ref[...],
                 preferred_element_type=jnp.float32)
    o_ref[...] = h1 + pe * jax.nn.sigmoid(g)


def _const_spec(shape):
    zeros = (0,) * len(shape)
    return pl.BlockSpec(shape, lambda i: zeros, pipeline_mode=pl.Buffered(1))


def _mixer(x, seq_len, pre_g, w_in, ln_g, ln_b, ws, bias_full, cw, cb, wax, b_a, b_x, lam, ga, gb):
    N, D = x.shape
    D2 = ln_g.shape[-1]
    NH = D2 // HEAD
    T = TILE
    assert N % (2 * T) == 0 and seq_len % (2 * T) == 0 and NCH * HEAD == 2 * LANES
    G = N // (2 * T)
    f32 = jnp.float32
    consts = [pre_g, w_in, ln_g, ln_b, ws, bias_full, cw, cb, wax, b_a, b_x, lam, ga, gb]
    kern = lambda *refs: _mixer_kernel(*refs, tiles_per_seq=seq_len // T)
    zbuf = [pltpu.VMEM((T, D2), f32)] * 4 + [pltpu.VMEM((NH, XB_ROWS, LANES), f32)]
    seg_buf = pltpu.VMEM((NH, SEG_ROWS, LANES), f32)
    return pl.pallas_call(
        kern,
        grid=(G + 1,),
        in_specs=[pl.BlockSpec((2 * T, D), lambda g: (jnp.minimum(g, G - 1), 0))]
        + [_const_spec(c.shape) for c in consts],
        out_specs=pl.BlockSpec((2 * T, 2 * D2), lambda g: (jnp.maximum(g - 1, 0), 0)),
        out_shape=jax.ShapeDtypeStruct((N, 2 * D2), jnp.bfloat16),
        scratch_shapes=zbuf + zbuf + [
            pltpu.VMEM((T, D), jnp.bfloat16),
            pltpu.VMEM((T, D), jnp.bfloat16),
            pltpu.VMEM((NH, HEAD, HEAD), jnp.bfloat16),
            pltpu.VMEM((2, D2), f32),
            pltpu.VMEM((CONV_WIDTH + 1, NH, SUBLANES, LANES), f32),
            pltpu.VMEM((T, D2), jnp.bfloat16),
            pltpu.VMEM((T, D2), f32),
            seg_buf,
            seg_buf,
            seg_buf,
            seg_buf,
            pltpu.VMEM((NH, SUBLANES, LANES), f32),
        ],
        compiler_params=pltpu.CompilerParams(
            dimension_semantics=("arbitrary",),
            vmem_limit_bytes=VMEM_LIMIT),
        name="mixer",
    )(x, *consts)


def _output(y, x, p, w_out, post_g, w_pe, w_pg):
    N, D = x.shape
    DP = p.shape[-1]
    T = OUT_TILE
    consts = [w_out, post_g, w_pe, w_pg]
    return pl.pallas_call(
        _output_kernel,
        grid=(N // T,),
        in_specs=[pl.BlockSpec((T, y.shape[-1]), lambda i: (i, 0)),
                  pl.BlockSpec((T, D), lambda i: (i, 0)),
                  pl.BlockSpec((T, DP), lambda i: (i, 0))]
        + [_const_spec(c.shape) for c in consts],
        out_specs=pl.BlockSpec((T, D), lambda i: (i, 0)),
        out_shape=jax.ShapeDtypeStruct((N, D), jnp.float32),
        compiler_params=pltpu.CompilerParams(
            dimension_semantics=("parallel",),
            vmem_limit_bytes=VMEM_LIMIT),
        name="output",
    )(y, x, p, *consts)


def kernel(x, p, pre_g, w_in, gmlp_ln_g, gmlp_ln_b, gmlp_ws, gmlp_bs, conv_w, conv_b, w_a, b_a, w_x, b_x, lam, gmlp_out_g, lru_out_g, w_out, post_g, w_pe, w_pg):
    B, S, D = x.shape
    depth = w_in.shape[0]
    D2 = gmlp_ln_g.shape[-1]
    bf16 = jnp.bfloat16
    row = lambda a: a.reshape(1, -1)
    h = x.reshape(B * S, D)
    for l in range(depth):
        bias_full = jnp.repeat(jnp.transpose(gmlp_bs[l]), HEAD, axis=1)
        wax = jnp.concatenate([w_a[l], w_x[l]], axis=-1).astype(bf16)
        y = _mixer(
            h, S, row(pre_g[l]), w_in[l].astype(bf16), row(gmlp_ln_g[l]), row(gmlp_ln_b[l]),
            gmlp_ws[l], bias_full, conv_w[l].reshape(CONV_WIDTH, D2), row(conv_b[l]),
            wax, row(b_a[l]), row(b_x[l]), row(lam[l]),
            row(gmlp_out_g[l]), row(lru_out_g[l]))
        h = _output(y, h, p[l].reshape(B * S, -1),
                    w_out[l].astype(bf16), row(post_g[l]), w_pe[l].astype(bf16),
                    w_pg[l].astype(bf16))
    return h.reshape(B, S, D)
```

```python
import collections
import math

import jax
import jax.numpy as jnp
from jax import lax
from jax.experimental import pallas as pl
from jax.experimental.pallas import tpu as pltpu

EPS = 1e-6
LRU_C = 8.0
CONV_WIDTH = 4
LANES = 128
SUBLANES = 8
HEAD = 128
VMEM_LIMIT = 60 * 1024 * 1024

TILE = 256
OUT_TILE = 2 * TILE
SEG = 34
SEG_ROWS = SUBLANES * SEG
XB_ROWS = SUBLANES + SEG_ROWS
NCH = TILE // HEAD
STRIP = 16
DOT_N = 256
GRP_U, GRP_V, GRP_GA, GRP_XB, GRP_GB = range(5)
DOT_ORDER = (GRP_XB, GRP_V, GRP_U, GRP_GA, GRP_GB)
PIECE_OPS = dict(gate=770, ends=816, scan=750, gmlp_v=930, gmlp_mix=50, gmlp_u=1180,
                 lru_out=610, norm=580)
SLOT_OPS = 1500

LOG2E = math.log2(math.e)
GELU_K1 = -2.0 * math.sqrt(2.0 / math.pi) * LOG2E
GELU_K2 = GELU_K1 * 0.044715

ZBuf = collections.namedtuple("ZBuf", "u v ga gb xb")


def _rms(x, g):
    return x * lax.rsqrt(jnp.mean(x * x, axis=-1, keepdims=True) + EPS) * g


def _gelu(x):
    return x / (1.0 + jnp.exp2(x * (GELU_K1 + GELU_K2 * (x * x))))


def _mixer_kernel(x_ref, pre_g_ref, w_in_ref, ln_g_ref, ln_b_ref, ws_ref, bias_ref,
                  cw_ref, cb_ref, wax_ref, ba_ref, bx_ref, lam_ref, ga_ref, gb_ref,
                  y_even_ref, y_odd_ref, *scratch, tiles_per_seq):
    z = (ZBuf(*scratch[0:5]), ZBuf(*scratch[5:10]))
    hn = scratch[10:12]
    wm_s, dec_s, cwb_s, vn_s, mix_s, xcp_s, a_s, b_s, h_s, hprev_s = scratch[12:]
    T = TILE
    D2 = z[0].u.shape[-1]
    NH = D2 // HEAD
    g = pl.program_id(0)

    @pl.when(g == 0)
    def _():
        row = lax.broadcasted_iota(jnp.int32, (HEAD, HEAD), 0)
        col = lax.broadcasted_iota(jnp.int32, (HEAD, HEAD), 1)
        for h in range(NH):
            hs = slice(h * HEAD, (h + 1) * HEAD)
            wm_s[h] = jnp.where(row >= col, ws_ref[h], 0.0).astype(jnp.bfloat16)
            for k in range(CONV_WIDTH):
                cwb_s[k, h] = jnp.broadcast_to(cw_ref[k:k + 1, hs], (SUBLANES, LANES))
            cwb_s[CONV_WIDTH, h] = jnp.broadcast_to(cb_ref[:, hs], (SUBLANES, LANES))
        dec = LRU_C * jax.nn.softplus(-lam_ref[...])
        dec_s[0:1, :] = dec
        dec_s[1:2, :] = dec * (-LOG2E)
        hn[1][...] = jnp.zeros_like(hn[1])
        z[0].xb[...] = jnp.zeros_like(z[0].xb)
        z[1].xb[...] = jnp.zeros_like(z[1].xb)
        hprev_s[...] = jnp.zeros_like(hprev_s)

    def strips(c, part):
        half = HEAD // 2
        r0 = c * HEAD + part * half
        return [slice(s0, s0 + STRIP) for s0 in range(r0, r0 + half, STRIP)]

    halves = [(c, part) for c in range(NCH) for part in range(2)]

    def norm_piece(row0, c, part, hn_s):
        def run():
            for r in strips(c, part):
                hn_s[r, :] = _rms(x_ref[row0 + r.start:row0 + r.stop, :],
                                  pre_g_ref[...]).astype(jnp.bfloat16)
        return run

    def dot_piece(hn_s, zb, col0):
        def run():
            res = jnp.dot(hn_s[...], w_in_ref[:, col0:col0 + DOT_N],
                          preferred_element_type=jnp.float32)
            grp, off = divmod(col0, D2)
            if grp == 3:
                for hh in range(DOT_N // HEAD):
                    zb.xb[off // HEAD + hh, SUBLANES:SUBLANES + T, :] = (
                        res[:, hh * HEAD:(hh + 1) * HEAD])
            else:
                dst = (zb.u, zb.v, zb.ga, None, zb.gb)[grp]
                dst[:, off:off + DOT_N] = res
        return run

    def gmlp_v_piece(zb, c, part):
        def run():
            for r in strips(c, part):
                v = _gelu(zb.v[r, :])
                mu = jnp.mean(v, axis=-1, keepdims=True)
                vc = v - mu
                vn = vc * lax.rsqrt(jnp.mean(vc * vc, axis=-1, keepdims=True) + EPS)
                vn_s[r, :] = (vn * ln_g_ref[...] + ln_b_ref[...]).astype(jnp.bfloat16)
        return run

    def gmlp_mix_piece(heads):
        def run():
            for h in heads:
                hs = slice(h * HEAD, (h + 1) * HEAD)
                rhs = jnp.concatenate(
                    [vn_s[c * HEAD:(c + 1) * HEAD, hs] for c in range(NCH)], axis=-1)
                m = jnp.dot(wm_s[h], rhs, preferred_element_type=jnp.float32)
                for c in range(NCH):
                    mix_s[c * HEAD:(c + 1) * HEAD, hs] = m[:, c * HEAD:(c + 1) * HEAD]
        return run

    def gmlp_u_piece(zb, yr, c, part):
        def run():
            for r in strips(c, part):
                mixed = mix_s[r, :] + bias_ref[r.start - c * HEAD:r.stop - c * HEAD, :]
                ya = _gelu(zb.u[r, :]) * mixed * jax.nn.silu(zb.ga[r, :])
                yr[r, 0:D2] = _rms(ya, ga_ref[...]).astype(yr.dtype)
        return run

    def gate_piece(zb, zb_next, heads, first):
        def run():
            rowi = lax.broadcasted_iota(jnp.int32, (SUBLANES, LANES), 0)
            for h in heads:
                hs = slice(h * HEAD, (h + 1) * HEAD)
                if first is not None:
                    zb.xb[h, 0:SUBLANES, :] = jnp.where(first, 0.0, zb.xb[h, 0:SUBLANES, :])
                zb_next.xb[h, 0:SUBLANES, :] = zb.xb[h, T:T + SUBLANES, :]
                taps = [zb.xb[h, pl.ds(SUBLANES - (CONV_WIDTH - 1) + j, SUBLANES, stride=SEG), :]
                        for j in range(CONV_WIDTH - 1)]
                for i in range(SEG):
                    taps.append(zb.xb[h, pl.ds(SUBLANES + i, SUBLANES, stride=SEG), :])
                    xc = cwb_s[CONV_WIDTH, h]
                    for k in range(CONV_WIDTH):
                        xc = xc + cwb_s[k, h] * taps[i + k]
                    xcp_s[h, i * SUBLANES:(i + 1) * SUBLANES, :] = xc
                gt = jnp.dot(xcp_s[h].astype(jnp.bfloat16), wax_ref[h],
                             preferred_element_type=jnp.float32)
                pad_i = T - (SUBLANES - 1) * SEG
                for i in range(SEG):
                    r = slice(i * SUBLANES, (i + 1) * SUBLANES)
                    rg = jax.nn.sigmoid(gt[r, :HEAD] + ba_ref[:, hs])
                    ig = jax.nn.sigmoid(gt[r, HEAD:] + bx_ref[:, hs])
                    a = jnp.exp2(rg * dec_s[1:2, hs])
                    th = jnp.tanh(rg * dec_s[0:1, hs])
                    mult = lax.rsqrt(0.5 + 0.5 / jnp.maximum(th, 1e-30))
                    gx = ig * xcp_s[h, r, :]
                    b = mult * gx
                    if i >= pad_i:
                        a = jnp.where(rowi == SUBLANES - 1, 1.0, a)
                        b = jnp.where(rowi == SUBLANES - 1, 0.0, b)
                    if i == 0 and first is not None:
                        b = jnp.where(jnp.logical_and(rowi == 0, first), gx, b)
                    a_s[h, r, :] = a
                    b_s[h, r, :] = b
        return run

    def scan_pieces(first):
        st = {}

        def ends():
            hfin = [jnp.zeros((SUBLANES, LANES), jnp.float32) for _ in range(NH)]
            afin = [jnp.ones((SUBLANES, LANES), jnp.float32) for _ in range(NH)]
            for i in range(SEG):
                r = slice(i * SUBLANES, (i + 1) * SUBLANES)
                for h in range(NH):
                    a = a_s[h, r, :]
                    hfin[h] = a * hfin[h] + b_s[h, r, :]
                    afin[h] = afin[h] * a
            st["hfin"], st["afin"] = hfin, afin

        def scan():
            hcur = []
            for h in range(NH):
                cst = hprev_s[h, 0:1, :]
                if first is not None:
                    cst = jnp.where(first, 0.0, cst)
                rows = []
                for s in range(SUBLANES):
                    rows.append(cst)
                    cst = st["afin"][h][s:s + 1, :] * cst + st["hfin"][h][s:s + 1, :]
                hprev_s[h, 0:1, :] = cst
                hcur.append(jnp.concatenate(rows, axis=0))
            for i in range(SEG):
                r = slice(i * SUBLANES, (i + 1) * SUBLANES)
                for h in range(NH):
                    hcur[h] = a_s[h, r, :] * hcur[h] + b_s[h, r, :]
                    h_s[h, pl.ds(i, SUBLANES, stride=SEG), :] = hcur[h]
        return ends, scan

    def lru_out_piece(zb, yr, c, part):
        def run():
            for r in strips(c, part):
                hh = jnp.concatenate([h_s[h, r, :] for h in range(NH)], axis=-1)
                yb = hh * jax.nn.silu(zb.gb[r, :])
                yr[r, D2:2 * D2] = _rms(yb, gb_ref[...]).astype(yr.dtype)
        return run

    def tile_work(zb, zb_other, hn_cur, yr, first):
        per_grp = D2 // DOT_N
        dots = [dot_piece(hn_cur, zb, grp * D2 + off)
                for grp in DOT_ORDER for off in range(0, D2, DOT_N)]
        first_of = lambda grp: DOT_ORDER.index(grp) * per_grp
        last_of = lambda *grps: max(first_of(grp) for grp in grps) + per_grp - 1
        ends, scan = scan_pieces(first)
        ops = PIECE_OPS
        vpu = ([(gate_piece(zb, zb_other, (h,), first), ops["gate"], last_of(GRP_XB))
                for h in range(NH)]
               + [(ends, ops["ends"], last_of(GRP_XB)), (scan, ops["scan"], last_of(GRP_XB))]
               + [(gmlp_v_piece(zb, c, part), ops["gmlp_v"], last_of(GRP_V)) for c, part in halves]
               + [(gmlp_mix_piece(range(h, h + NH // 2)), ops["gmlp_mix"], last_of(GRP_V))
                  for h in range(0, NH, NH // 2)]
               + [(gmlp_u_piece(zb, yr, c, part), ops["gmlp_u"], last_of(GRP_V, GRP_U, GRP_GA))
                  for c, part in halves]
               + [(lru_out_piece(zb, yr, c, part), ops["lru_out"], last_of(GRP_XB, GRP_GB))
                  for c, part in halves])
        assert all(a[2] <= b[2] for a, b in zip(vpu, vpu[1:])), "pieces are emitted in list order"
        return dots, vpu

    tiles = [
        dict(zb=z[1], other=z[0], hn_cur=hn[1], yr=y_odd_ref, first=None,
             norm=[norm_piece(0, c, part, hn[0]) for c, part in halves]),
        dict(zb=z[0], other=z[1], hn_cur=hn[0], yr=y_even_ref,
             first=(2 * g) % tiles_per_seq == 0,
             norm=[norm_piece(T, c, part, hn[1]) for c, part in halves]),
    ]
    queue, emitted, room = [], 0, 0
    for t in tiles:
        dots, vpu = tile_work(t["zb"], t["other"], t["hn_cur"], t["yr"], t["first"])
        queue += [(run, n, emitted + dep + 1) for run, n, dep in vpu]
        fill = list(t["norm"])
        for dot in dots:
            dot()
            emitted += 1
            room = SLOT_OPS
            while queue and queue[0][2] <= emitted and room > 0:
                run, n, _ = queue.pop(0)
                run()
                room -= n
            while fill and room > 0:
                fill.pop(0)()
                room -= PIECE_OPS["norm"]
        for run in fill:
            run()
    for run, _, _ in queue:
        run()


def _output_kernel(y_even_ref, y_odd_ref, x_ref, p_ref, w_out_ref, post_g_ref, w_pe_ref,
                   w_pg_ref, o_ref):
    y = jnp.concatenate([y_even_ref[...], y_odd_ref[...]], axis=0)
    o = jnp.dot(y, w_out_ref[...], preferred_element_type=jnp.float32)
    h1 = x_ref[...] + _rms(o, post_g_ref[...])
    g = jnp.dot(h1.astype(jnp.bfloat16), w_pg_ref[...], preferred_element_type=jnp.float32)
    pe = jnp.dot(p_ref[...].astype(jnp.bfloat16), w_pe_ref[...],
                 preferred_element_type=jnp.float32)
    o_ref[...] = h1 + pe * jax.nn.sigmoid(g)


def _const_spec(shape):
    zeros = (0,) * len(shape)
    return pl.BlockSpec(shape, lambda i: zeros, pipeline_mode=pl.Buffered(1))


def _mixer(x, seq_len, pre_g, w_in, ln_g, ln_b, ws, bias_full, cw, cb, wax, b_a, b_x, lam, ga, gb):
    N, D = x.shape
    D2 = ln_g.shape[-1]
    NH = D2 // HEAD
    T = TILE
    assert N % (2 * T) == 0 and seq_len % (2 * T) == 0 and NCH * HEAD == 2 * LANES
    G = N // (2 * T)
    f32 = jnp.float32
    consts = [pre_g, w_in, ln_g, ln_b, ws, bias_full, cw, cb, wax, b_a, b_x, lam, ga, gb]
    kern = lambda *refs: _mixer_kernel(*refs, tiles_per_seq=seq_len // T)
    zbuf = [pltpu.VMEM((T, D2), f32)] * 4 + [pltpu.VMEM((NH, XB_ROWS, LANES), f32)]
    seg_buf = pltpu.VMEM((NH, SEG_ROWS, LANES), f32)
    return pl.pallas_call(
        kern,
        grid=(G + 1,),
        in_specs=[pl.BlockSpec((2 * T, D), lambda g: (jnp.minimum(g, G - 1), 0))]
        + [_const_spec(c.shape) for c in consts],
        out_specs=[pl.BlockSpec((None, T, 2 * D2), lambda g: (g, 0, 0)),
                   pl.BlockSpec((None, T, 2 * D2), lambda g: (jnp.maximum(g - 1, 0), 0, 0))],
        out_shape=[jax.ShapeDtypeStruct((G + 1, T, 2 * D2), jnp.bfloat16),
                   jax.ShapeDtypeStruct((G, T, 2 * D2), jnp.bfloat16)],
        scratch_shapes=zbuf + zbuf + [
            pltpu.VMEM((T, D), jnp.bfloat16),
            pltpu.VMEM((T, D), jnp.bfloat16),
            pltpu.VMEM((NH, HEAD, HEAD), jnp.bfloat16),
            pltpu.VMEM((2, D2), f32),
            pltpu.VMEM((CONV_WIDTH + 1, NH, SUBLANES, LANES), f32),
            pltpu.VMEM((T, D2), jnp.bfloat16),
            pltpu.VMEM((T, D2), f32),
            seg_buf,
            seg_buf,
            seg_buf,
            seg_buf,
            pltpu.VMEM((NH, SUBLANES, LANES), f32),
        ],
        compiler_params=pltpu.CompilerParams(
            dimension_semantics=("arbitrary",),
            vmem_limit_bytes=VMEM_LIMIT),
        name="mixer",
    )(x, *consts)


def _output(y_even, y_odd, x, p, w_out, post_g, w_pe, w_pg):
    N, D = x.shape
    DP = p.shape[-1]
    T = OUT_TILE
    consts = [w_out, post_g, w_pe, w_pg]
    y_spec = pl.BlockSpec((None, TILE, y_even.shape[-1]), lambda i: (i, 0, 0))
    return pl.pallas_call(
        _output_kernel,
        grid=(N // T,),
        in_specs=[y_spec, y_spec,
                  pl.BlockSpec((T, D), lambda i: (i, 0)),
                  pl.BlockSpec((T, DP), lambda i: (i, 0))]
        + [_const_spec(c.shape) for c in consts],
        out_specs=pl.BlockSpec((T, D), lambda i: (i, 0)),
        out_shape=jax.ShapeDtypeStruct((N, D), jnp.float32),
        compiler_params=pltpu.CompilerParams(
            dimension_semantics=("parallel",),
            vmem_limit_bytes=VMEM_LIMIT),
        name="output",
    )(y_even, y_odd, x, p, *consts)


def kernel(x, p, pre_g, w_in, gmlp_ln_g, gmlp_ln_b, gmlp_ws, gmlp_bs, conv_w, conv_b, w_a, b_a, w_x, b_x, lam, gmlp_out_g, lru_out_g, w_out, post_g, w_pe, w_pg):
    B, S, D = x.shape
    depth = w_in.shape[0]
    D2 = gmlp_ln_g.shape[-1]
    bf16 = jnp.bfloat16
    row = lambda a: a.reshape(1, -1)
    h = x.reshape(B * S, D)
    for l in range(depth):
        bias_full = jnp.repeat(jnp.transpose(gmlp_bs[l]), HEAD, axis=1)
        wax = jnp.concatenate([w_a[l], w_x[l]], axis=-1).astype(bf16)
        y_even, y_odd = _mixer(
            h, S, row(pre_g[l]), w_in[l].astype(bf16), row(gmlp_ln_g[l]), row(gmlp_ln_b[l]),
            gmlp_ws[l], bias_full, conv_w[l].reshape(CONV_WIDTH, D2), row(conv_b[l]),
            wax, row(b_a[l]), row(b_x[l]), row(lam[l]),
            row(gmlp_out_g[l]), row(lru_out_g[l]))
        h = _output(y_even, y_odd, h, p[l].reshape(B * S, -1),
                    w_out[l].astype(bf16), row(post_g[l]), w_pe[l].astype(bf16),
                    w_pg[l].astype(bf16))
    return h.reshape(B, S, D)
```

```python
import collections
import math

import jax
import jax.numpy as jnp
from jax import lax
from jax.experimental import pallas as pl
from jax.experimental.pallas import tpu as pltpu

EPS = 1e-6
LRU_C = 8.0
CONV_WIDTH = 4
LANES = 128
SUBLANES = 8
HEAD = 128
VMEM_LIMIT = 60 * 1024 * 1024

TILE = 256
OUT_TILE = 2 * TILE
SEG = 34
SEG_ROWS = SUBLANES * SEG
XB_ROWS = SUBLANES + SEG_ROWS
NCH = TILE // HEAD
STRIP = 16
DOT_N = 256
GRP_U, GRP_V, GRP_GA, GRP_XB, GRP_GB = range(5)
DOT_ORDER = (GRP_XB, GRP_V, GRP_U, GRP_GA, GRP_GB)
PIECE_OPS = dict(gate=770, ends=816, scan=750, gmlp_v=930, gmlp_mix=50, gmlp_u=1180,
                 lru_out=610, norm=580)
SLOT_OPS = 1500

LOG2E = math.log2(math.e)
GELU_K1 = -2.0 * math.sqrt(2.0 / math.pi) * LOG2E
GELU_K2 = GELU_K1 * 0.044715

ZBuf = collections.namedtuple("ZBuf", "u v ga gb xb")


def _rms(x, g):
    return x * lax.rsqrt(jnp.mean(x * x, axis=-1, keepdims=True) + EPS) * g


def _sigmoid(x):
    return 0.5 * jnp.tanh(0.5 * x) + 0.5


def _silu(x):
    hx = 0.5 * x
    return hx * jnp.tanh(hx) + hx


def _gelu(x):
    return x / (1.0 + jnp.exp2(x * (GELU_K1 + GELU_K2 * (x * x))))


def _mixer_kernel(x_ref, pre_g_ref, w_in_ref, ln_g_ref, ln_b_ref, ws_ref, bias_ref,
                  cw_ref, cb_ref, wax_ref, ba_ref, bx_ref, lam_ref, ga_ref, gb_ref,
                  w_out_f32_ref, w_pg_f32_ref,
                  y_even_ref, y_odd_ref, w_out_bf16_ref, w_pg_bf16_ref, *scratch, tiles_per_seq):
    w_out_bf16_ref[...] = w_out_f32_ref[...].astype(jnp.bfloat16)
    w_pg_bf16_ref[...] = w_pg_f32_ref[...].astype(jnp.bfloat16)

    z = (ZBuf(*scratch[0:5]), ZBuf(*scratch[5:10]))
    hn = scratch[10:12]
    wm_s, dec_s, cwb_s, vn_s, mix_s, xcp_s, a_s, b_s, h_s, hprev_s = scratch[12:]
    T = TILE
    D2 = z[0].u.shape[-1]
    NH = D2 // HEAD
    g = pl.program_id(0)

    @pl.when(g == 0)
    def _():
        row = lax.broadcasted_iota(jnp.int32, (HEAD, HEAD), 0)
        col = lax.broadcasted_iota(jnp.int32, (HEAD, HEAD), 1)
        for h in range(NH):
            hs = slice(h * HEAD, (h + 1) * HEAD)
            wm_s[h] = jnp.where(row >= col, ws_ref[h], 0.0).astype(jnp.bfloat16)
            for k in range(CONV_WIDTH):
                cwb_s[k, h] = jnp.broadcast_to(cw_ref[k:k + 1, hs], (SUBLANES, LANES))
            cwb_s[CONV_WIDTH, h] = jnp.broadcast_to(cb_ref[:, hs], (SUBLANES, LANES))
        dec = LRU_C * jax.nn.softplus(-lam_ref[...])
        dec_s[0:1, :] = dec
        dec_s[1:2, :] = dec * (-LOG2E)
        hn[1][...] = jnp.zeros_like(hn[1])
        z[0].xb[...] = jnp.zeros_like(z[0].xb)
        z[1].xb[...] = jnp.zeros_like(z[1].xb)
        hprev_s[...] = jnp.zeros_like(hprev_s)

    def strips(c, part):
        half = HEAD // 2
        r0 = c * HEAD + part * half
        return [slice(s0, s0 + STRIP) for s0 in range(r0, r0 + half, STRIP)]

    halves = [(c, part) for c in range(NCH) for part in range(2)]

    def norm_piece(row0, c, part, hn_s):
        def run():
            for r in strips(c, part):
                hn_s[r, :] = _rms(x_ref[row0 + r.start:row0 + r.stop, :],
                                  pre_g_ref[...]).astype(jnp.bfloat16)
        return run

    def dot_piece(hn_s, zb, col0):
        def run():
            res = jnp.dot(hn_s[...], w_in_ref[:, col0:col0 + DOT_N],
                          preferred_element_type=jnp.float32)
            grp, off = divmod(col0, D2)
            if grp == GRP_XB:
                for hh in range(DOT_N // HEAD):
                    zb.xb[off // HEAD + hh, SUBLANES:SUBLANES + T, :] = (
                        res[:, hh * HEAD:(hh + 1) * HEAD])
            else:
                dst = {GRP_U: zb.u, GRP_V: zb.v, GRP_GA: zb.ga, GRP_GB: zb.gb}[grp]
                dst[:, off:off + DOT_N] = res
        return run

    def gmlp_v_piece(zb, c, part):
        def run():
            for r in strips(c, part):
                v = _gelu(zb.v[r, :])
                mu = jnp.mean(v, axis=-1, keepdims=True)
                vc = v - mu
                vn = vc * lax.rsqrt(jnp.mean(vc * vc, axis=-1, keepdims=True) + EPS)
                vn_s[r, :] = (vn * ln_g_ref[...] + ln_b_ref[...]).astype(jnp.bfloat16)
        return run

    def gmlp_mix_piece(heads):
        def run():
            for h in heads:
                hs = slice(h * HEAD, (h + 1) * HEAD)
                rhs = jnp.concatenate(
                    [vn_s[c * HEAD:(c + 1) * HEAD, hs] for c in range(NCH)], axis=-1)
                m = jnp.dot(wm_s[h], rhs, preferred_element_type=jnp.float32)
                for c in range(NCH):
                    mix_s[c * HEAD:(c + 1) * HEAD, hs] = m[:, c * HEAD:(c + 1) * HEAD]
        return run

    def gmlp_u_piece(zb, yr, c, part):
        def run():
            for r in strips(c, part):
                mixed = mix_s[r, :] + bias_ref[r.start - c * HEAD:r.stop - c * HEAD, :]
                ya = _gelu(zb.u[r, :]) * mixed * _silu(zb.ga[r, :])
                yr[r, 0:D2] = _rms(ya, ga_ref[...]).astype(yr.dtype)
        return run

    def gate_piece(zb, zb_next, h, first):
        def run():
            rowi = lax.broadcasted_iota(jnp.int32, (SUBLANES, LANES), 0)
            hs = slice(h * HEAD, (h + 1) * HEAD)
            if first is not None:
                zb.xb[h, 0:SUBLANES, :] = jnp.where(first, 0.0, zb.xb[h, 0:SUBLANES, :])
            zb_next.xb[h, 0:SUBLANES, :] = zb.xb[h, T:T + SUBLANES, :]
            taps = [zb.xb[h, pl.ds(SUBLANES - (CONV_WIDTH - 1) + j, SUBLANES, stride=SEG), :]
                    for j in range(CONV_WIDTH - 1)]
            for i in range(SEG):
                taps.append(zb.xb[h, pl.ds(SUBLANES + i, SUBLANES, stride=SEG), :])
                xc = cwb_s[CONV_WIDTH, h]
                for k in range(CONV_WIDTH):
                    xc = xc + cwb_s[k, h] * taps[i + k]
                xcp_s[h, i * SUBLANES:(i + 1) * SUBLANES, :] = xc
            gt = jnp.dot(xcp_s[h].astype(jnp.bfloat16), wax_ref[h],
                         preferred_element_type=jnp.float32)
            pad_i = T - (SUBLANES - 1) * SEG
            for i in range(SEG):
                r = slice(i * SUBLANES, (i + 1) * SUBLANES)
                rg = _sigmoid(gt[r, :HEAD] + ba_ref[:, hs])
                ig = _sigmoid(gt[r, HEAD:] + bx_ref[:, hs])
                a = jnp.exp2(rg * dec_s[1:2, hs])
                th = jnp.tanh(rg * dec_s[0:1, hs])
                mult = lax.rsqrt(0.5 + 0.5 / jnp.maximum(th, 1e-30))
                gx = ig * xcp_s[h, r, :]
                b = mult * gx
                if i >= pad_i:
                    a = jnp.where(rowi == SUBLANES - 1, 1.0, a)
                    b = jnp.where(rowi == SUBLANES - 1, 0.0, b)
                if i == 0 and first is not None:
                    b = jnp.where(jnp.logical_and(rowi == 0, first), gx, b)
                a_s[h, r, :] = a
                b_s[h, r, :] = b
        return run

    def scan_pieces(first):
        st = {}

        def ends():
            hfin = [jnp.zeros((SUBLANES, LANES), jnp.float32) for _ in range(NH)]
            afin = [jnp.ones((SUBLANES, LANES), jnp.float32) for _ in range(NH)]
            for i in range(SEG):
                r = slice(i * SUBLANES, (i + 1) * SUBLANES)
                for h in range(NH):
                    a = a_s[h, r, :]
                    hfin[h] = a * hfin[h] + b_s[h, r, :]
                    afin[h] = afin[h] * a
            st["hfin"], st["afin"] = hfin, afin

        def scan():
            hcur = []
            for h in range(NH):
                cst = hprev_s[h, 0:1, :]
                if first is not None:
                    cst = jnp.where(first, 0.0, cst)
                rows = []
                for s in range(SUBLANES):
                    rows.append(cst)
                    cst = st["afin"][h][s:s + 1, :] * cst + st["hfin"][h][s:s + 1, :]
                hprev_s[h, 0:1, :] = cst
                hcur.append(jnp.concatenate(rows, axis=0))
            for i in range(SEG):
                r = slice(i * SUBLANES, (i + 1) * SUBLANES)
                for h in range(NH):
                    hcur[h] = a_s[h, r, :] * hcur[h] + b_s[h, r, :]
                    h_s[h, pl.ds(i, SUBLANES, stride=SEG), :] = hcur[h]
        return ends, scan

    def lru_out_piece(zb, yr, c, part):
        def run():
            for r in strips(c, part):
                hh = jnp.concatenate([h_s[h, r, :] for h in range(NH)], axis=-1)
                yb = hh * _silu(zb.gb[r, :])
                yr[r, D2:2 * D2] = _rms(yb, gb_ref[...]).astype(yr.dtype)
        return run

    def tile_work(zb, zb_other, hn_cur, yr, first):
        per_grp = D2 // DOT_N
        dots = [dot_piece(hn_cur, zb, grp * D2 + off)
                for grp in DOT_ORDER for off in range(0, D2, DOT_N)]
        first_of = lambda grp: DOT_ORDER.index(grp) * per_grp
        last_of = lambda *grps: max(first_of(grp) for grp in grps) + per_grp - 1
        ends, scan = scan_pieces(first)
        ops = PIECE_OPS
        vpu = ([(gate_piece(zb, zb_other, h, first), ops["gate"], last_of(GRP_XB))
                for h in range(NH)]
               + [(ends, ops["ends"], last_of(GRP_XB)), (scan, ops["scan"], last_of(GRP_XB))]
               + [(gmlp_v_piece(zb, c, part), ops["gmlp_v"], last_of(GRP_V)) for c, part in halves]
               + [(gmlp_mix_piece(range(h, h + NH // 2)), ops["gmlp_mix"], last_of(GRP_V))
                  for h in range(0, NH, NH // 2)]
               + [(gmlp_u_piece(zb, yr, c, part), ops["gmlp_u"], last_of(GRP_V, GRP_U, GRP_GA))
                  for c, part in halves]
               + [(lru_out_piece(zb, yr, c, part), ops["lru_out"], last_of(GRP_XB, GRP_GB))
                  for c, part in halves])
        assert all(a[2] <= b[2] for a, b in zip(vpu, vpu[1:])), "pieces are emitted in list order"
        return dots, vpu

    tiles = [
        dict(zb=z[1], other=z[0], hn_cur=hn[1], yr=y_odd_ref, first=None,
             norm=[norm_piece(0, c, part, hn[0]) for c, part in halves]),
        dict(zb=z[0], other=z[1], hn_cur=hn[0], yr=y_even_ref,
             first=(2 * g) % tiles_per_seq == 0,
             norm=[norm_piece(T, c, part, hn[1]) for c, part in halves]),
    ]
    queue, emitted = [], 0
    for t in tiles:
        dots, vpu = tile_work(t["zb"], t["other"], t["hn_cur"], t["yr"], t["first"])
        queue += [(run, n, emitted + dep + 1) for run, n, dep in vpu]
        fill = list(t["norm"])
        for dot in dots:
            dot()
            emitted += 1
            room = SLOT_OPS
            while queue and queue[0][2] <= emitted and room > 0:
                run, n, _ = queue.pop(0)
                run()
                room -= n
            while fill and room > 0:
                fill.pop(0)()
                room -= PIECE_OPS["norm"]
        for run in fill:
            run()
    for run, _, _ in queue:
        run()


def _output_kernel(y_even_ref, y_odd_ref, x_ref, p_ref, w_out_ref, post_g_ref, w_pe_ref,
                   w_pg_ref, o_ref):
    bf16 = jnp.bfloat16
    rows = (slice(0, TILE), slice(TILE, 2 * TILE))
    o = [jnp.dot(y_ref[...], w_out_ref[...], preferred_element_type=jnp.float32)
         for y_ref in (y_even_ref, y_odd_ref)]
    pe = [jnp.dot(p_ref[r, :].astype(bf16), w_pe_ref[...], preferred_element_type=jnp.float32)
          for r in rows]
    h1 = [x_ref[r, :] + _rms(o_h, post_g_ref[...]) for r, o_h in zip(rows, o)]
    gate = [jnp.dot(h.astype(bf16), w_pg_ref[...], preferred_element_type=jnp.float32)
            for h in h1]
    for r, h, pe_h, g_h in zip(rows, h1, pe, gate):
        o_ref[r, :] = h + pe_h * _sigmoid(g_h)


def _const_spec(shape):
    zeros = (0,) * len(shape)
    return pl.BlockSpec(shape, lambda i: zeros, pipeline_mode=pl.Buffered(1))


def _mixer(x, seq_len, pre_g, w_in, ln_g, ln_b, ws, bias_full, cw, cb, wax, b_a, b_x, lam, ga, gb,
           w_out, w_pg):
    N, D = x.shape
    D2 = ln_g.shape[-1]
    NH = D2 // HEAD
    T = TILE
    assert N % (2 * T) == 0 and seq_len % (2 * T) == 0 and NCH * HEAD == 2 * LANES
    G = N // (2 * T)
    f32 = jnp.float32
    consts = [pre_g, w_in, ln_g, ln_b, ws, bias_full, cw, cb, wax, b_a, b_x, lam, ga, gb]
    kern = lambda *refs: _mixer_kernel(*refs, tiles_per_seq=seq_len // T)
    zbuf = [pltpu.VMEM((T, D2), f32)] * 4 + [pltpu.VMEM((NH, XB_ROWS, LANES), f32)]
    seg_buf = pltpu.VMEM((NH, SEG_ROWS, LANES), f32)
    assert w_out.shape == w_pg.shape and w_out.shape[0] % (G * STRIP) == 0
    slab_spec = pl.BlockSpec((w_out.shape[0] // G, w_out.shape[1]),
                             lambda g: (jnp.minimum(g, G - 1), 0))
    return pl.pallas_call(
        kern,
        grid=(G + 1,),
        in_specs=[pl.BlockSpec((2 * T, D), lambda g: (jnp.minimum(g, G - 1), 0))]
        + [_const_spec(c.shape) for c in consts] + [slab_spec, slab_spec],
        out_specs=[pl.BlockSpec((None, T, 2 * D2), lambda g: (g, 0, 0)),
                   pl.BlockSpec((None, T, 2 * D2), lambda g: (jnp.maximum(g - 1, 0), 0, 0)),
                   slab_spec, slab_spec],
        out_shape=[jax.ShapeDtypeStruct((G + 1, T, 2 * D2), jnp.bfloat16),
                   jax.ShapeDtypeStruct((G, T, 2 * D2), jnp.bfloat16),
                   jax.ShapeDtypeStruct(w_out.shape, jnp.bfloat16),
                   jax.ShapeDtypeStruct(w_pg.shape, jnp.bfloat16)],
        scratch_shapes=zbuf + zbuf + [
            pltpu.VMEM((T, D), jnp.bfloat16),
            pltpu.VMEM((T, D), jnp.bfloat16),
            pltpu.VMEM((NH, HEAD, HEAD), jnp.bfloat16),
            pltpu.VMEM((2, D2), f32),
            pltpu.VMEM((CONV_WIDTH + 1, NH, SUBLANES, LANES), f32),
            pltpu.VMEM((T, D2), jnp.bfloat16),
            pltpu.VMEM((T, D2), f32),
            seg_buf,
            seg_buf,
            seg_buf,
            seg_buf,
            pltpu.VMEM((NH, SUBLANES, LANES), f32),
        ],
        compiler_params=pltpu.CompilerParams(
            dimension_semantics=("arbitrary",),
            vmem_limit_bytes=VMEM_LIMIT),
        name="mixer",
    )(x, *consts, w_out, w_pg)


def _output(y_even, y_odd, x, p, w_out, post_g, w_pe, w_pg):
    N, D = x.shape
    DP = p.shape[-1]
    T = OUT_TILE
    consts = [w_out, post_g, w_pe, w_pg]
    y_spec = pl.BlockSpec((None, TILE, y_even.shape[-1]), lambda i: (i, 0, 0))
    return pl.pallas_call(
        _output_kernel,
        grid=(N // T,),
        in_specs=[y_spec, y_spec,
                  pl.BlockSpec((T, D), lambda i: (i, 0)),
                  pl.BlockSpec((T, DP), lambda i: (i, 0))]
        + [_const_spec(c.shape) for c in consts],
        out_specs=pl.BlockSpec((T, D), lambda i: (i, 0)),
        out_shape=jax.ShapeDtypeStruct((N, D), jnp.float32),
        compiler_params=pltpu.CompilerParams(
            dimension_semantics=("parallel",),
            vmem_limit_bytes=VMEM_LIMIT),
        name="output",
    )(y_even, y_odd, x, p, *consts)


def kernel(x, p, pre_g, w_in, gmlp_ln_g, gmlp_ln_b, gmlp_ws, gmlp_bs, conv_w, conv_b, w_a, b_a, w_x, b_x, lam, gmlp_out_g, lru_out_g, w_out, post_g, w_pe, w_pg):
    B, S, D = x.shape
    depth = w_in.shape[0]
    D2 = gmlp_ln_g.shape[-1]
    bf16 = jnp.bfloat16
    row = lambda a: a.reshape(1, -1)
    h = x.reshape(B * S, D)
    for l in range(depth):
        bias_full = jnp.repeat(jnp.transpose(gmlp_bs[l]), HEAD, axis=1)
        wax = jnp.concatenate([w_a[l], w_x[l]], axis=-1).astype(bf16)
        y_even, y_odd, w_out_bf16, w_pg_bf16 = _mixer(
            h, S, row(pre_g[l]), w_in[l].astype(bf16), row(gmlp_ln_g[l]), row(gmlp_ln_b[l]),
            gmlp_ws[l], bias_full, conv_w[l].reshape(CONV_WIDTH, D2), row(conv_b[l]),
            wax, row(b_a[l]), row(b_x[l]), row(lam[l]),
            row(gmlp_out_g[l]), row(lru_out_g[l]), w_out[l], w_pg[l])
        h = _output(y_even, y_odd, h, p[l].reshape(B * S, -1),
                    w_out_bf16, row(post_g[l]), w_pe[l].astype(bf16), w_pg_bf16)
    return h.reshape(B, S, D)
```

```python
import collections
import math

import jax
import jax.numpy as jnp
from jax import lax
from jax.experimental import pallas as pl
from jax.experimental.pallas import tpu as pltpu

EPS = 1e-6
LRU_C = 8.0
CONV_WIDTH = 4
LANES = 128
SUBLANES = 8
HEAD = 128
VMEM_LIMIT = 60 * 1024 * 1024

TILE = 256
OUT_TILE = 2 * TILE
SEG = 34
SEG_ROWS = SUBLANES * SEG
XB_ROWS = SUBLANES + SEG_ROWS
NCH = TILE // HEAD
STRIP = 16
DOT_N = 256
GRP_U, GRP_V, GRP_GA, GRP_XB, GRP_GB = range(5)
DOT_ORDER = (GRP_XB, GRP_V, GRP_U, GRP_GA, GRP_GB)
PIECE_OPS = dict(gate=770, ends=816, scan=750, gmlp_v=930, gmlp_mix=50, gmlp_u=1180,
                 lru_out=610, norm=580)
SLOT_OPS = 1500

LOG2E = math.log2(math.e)
GELU_K1 = -2.0 * math.sqrt(2.0 / math.pi) * LOG2E
GELU_K2 = GELU_K1 * 0.044715

ZBuf = collections.namedtuple("ZBuf", "u v ga gb xb")


def _rms(x, g):
    return x * lax.rsqrt(jnp.mean(x * x, axis=-1, keepdims=True) + EPS) * g


def _sigmoid(x):
    return 0.5 * jnp.tanh(0.5 * x) + 0.5


def _silu(x):
    hx = 0.5 * x
    return hx * jnp.tanh(hx) + hx


def _gelu(x):
    return x / (1.0 + jnp.exp2(x * (GELU_K1 + GELU_K2 * (x * x))))


def _mixer_kernel(x_ref, pre_g_ref, w_in_ref, ln_g_ref, ln_b_ref, ws_ref, bias_ref,
                  cw_ref, cb_ref, wax_ref, ba_ref, bx_ref, lam_ref, ga_ref, gb_ref,
                  w_out_f32_ref, w_pg_f32_ref,
                  y_even_ref, y_odd_ref, w_out_bf16_ref, w_pg_bf16_ref, *scratch, tiles_per_seq,
                  n_steps):
    w_out_bf16_ref[...] = w_out_f32_ref[...].astype(jnp.bfloat16)
    w_pg_bf16_ref[...] = w_pg_f32_ref[...].astype(jnp.bfloat16)

    z = (ZBuf(*scratch[0:5]), ZBuf(*scratch[5:10]))
    hn = scratch[10:12]
    wm_s, dec_s, cwb_s, vn_s, mix_s, xcp_s, a_s, b_s, h_s, hprev_s = scratch[12:]
    T = TILE
    D2 = z[0].u.shape[-1]
    NH = D2 // HEAD
    g = pl.program_id(0)

    @pl.when(g == 0)
    def _():
        row = lax.broadcasted_iota(jnp.int32, (HEAD, HEAD), 0)
        col = lax.broadcasted_iota(jnp.int32, (HEAD, HEAD), 1)
        for h in range(NH):
            hs = slice(h * HEAD, (h + 1) * HEAD)
            wm_s[h] = jnp.where(row >= col, ws_ref[h], 0.0).astype(jnp.bfloat16)
            for k in range(CONV_WIDTH):
                cwb_s[k, h] = jnp.broadcast_to(cw_ref[k:k + 1, hs], (SUBLANES, LANES))
            cwb_s[CONV_WIDTH, h] = jnp.broadcast_to(cb_ref[:, hs], (SUBLANES, LANES))
        dec = LRU_C * jax.nn.softplus(-lam_ref[...])
        dec_s[0:1, :] = dec
        dec_s[1:2, :] = dec * (-LOG2E)
        z[0].xb[...] = jnp.zeros_like(z[0].xb)
        z[1].xb[...] = jnp.zeros_like(z[1].xb)
        hprev_s[...] = jnp.zeros_like(hprev_s)

    def strips(c, part):
        half = HEAD // 2
        r0 = c * HEAD + part * half
        return [slice(s0, s0 + STRIP) for s0 in range(r0, r0 + half, STRIP)]

    halves = [(c, part) for c in range(NCH) for part in range(2)]

    def norm_piece(row0, c, part, hn_s):
        def run():
            for r in strips(c, part):
                hn_s[r, :] = _rms(x_ref[row0 + r.start:row0 + r.stop, :],
                                  pre_g_ref[...]).astype(jnp.bfloat16)
        return run

    def dot_piece(hn_s, zb, col0):
        def run():
            res = jnp.dot(hn_s[...], w_in_ref[:, col0:col0 + DOT_N],
                          preferred_element_type=jnp.float32)
            grp, off = divmod(col0, D2)
            if grp == GRP_XB:
                for hh in range(DOT_N // HEAD):
                    zb.xb[off // HEAD + hh, SUBLANES:SUBLANES + T, :] = (
                        res[:, hh * HEAD:(hh + 1) * HEAD])
            else:
                dst = {GRP_U: zb.u, GRP_V: zb.v, GRP_GA: zb.ga, GRP_GB: zb.gb}[grp]
                dst[:, off:off + DOT_N] = res
        return run

    def gmlp_v_piece(zb, c, part):
        def run():
            for r in strips(c, part):
                v = _gelu(zb.v[r, :])
                mu = jnp.mean(v, axis=-1, keepdims=True)
                vc = v - mu
                vn = vc * lax.rsqrt(jnp.mean(vc * vc, axis=-1, keepdims=True) + EPS)
                vn_s[r, :] = (vn * ln_g_ref[...] + ln_b_ref[...]).astype(jnp.bfloat16)
        return run

    def gmlp_mix_piece(heads):
        def run():
            for h in heads:
                hs = slice(h * HEAD, (h + 1) * HEAD)
                rhs = jnp.concatenate(
                    [vn_s[c * HEAD:(c + 1) * HEAD, hs] for c in range(NCH)], axis=-1)
                m = jnp.dot(wm_s[h], rhs, preferred_element_type=jnp.float32)
                for c in range(NCH):
                    mix_s[c * HEAD:(c + 1) * HEAD, hs] = m[:, c * HEAD:(c + 1) * HEAD]
        return run

    def gmlp_u_piece(zb, yr, c, part):
        def run():
            for r in strips(c, part):
                mixed = mix_s[r, :] + bias_ref[r.start - c * HEAD:r.stop - c * HEAD, :]
                ya = _gelu(zb.u[r, :]) * mixed * _silu(zb.ga[r, :])
                yr[r, 0:D2] = _rms(ya, ga_ref[...]).astype(yr.dtype)
        return run

    def gate_piece(zb, zb_next, h, first):
        def run():
            rowi = lax.broadcasted_iota(jnp.int32, (SUBLANES, LANES), 0)
            hs = slice(h * HEAD, (h + 1) * HEAD)
            if first is not None:
                zb.xb[h, 0:SUBLANES, :] = jnp.where(first, 0.0, zb.xb[h, 0:SUBLANES, :])
            zb_next.xb[h, 0:SUBLANES, :] = zb.xb[h, T:T + SUBLANES, :]
            taps = [zb.xb[h, pl.ds(SUBLANES - (CONV_WIDTH - 1) + j, SUBLANES, stride=SEG), :]
                    for j in range(CONV_WIDTH - 1)]
            for i in range(SEG):
                taps.append(zb.xb[h, pl.ds(SUBLANES + i, SUBLANES, stride=SEG), :])
                xc = cwb_s[CONV_WIDTH, h]
                for k in range(CONV_WIDTH):
                    xc = xc + cwb_s[k, h] * taps[i + k]
                xcp_s[h, i * SUBLANES:(i + 1) * SUBLANES, :] = xc
            gt = jnp.dot(xcp_s[h].astype(jnp.bfloat16), wax_ref[h],
                         preferred_element_type=jnp.float32)
            pad_i = T - (SUBLANES - 1) * SEG
            for i in range(SEG):
                r = slice(i * SUBLANES, (i + 1) * SUBLANES)
                rg = _sigmoid(gt[r, :HEAD] + ba_ref[:, hs])
                ig = _sigmoid(gt[r, HEAD:] + bx_ref[:, hs])
                a = jnp.exp2(rg * dec_s[1:2, hs])
                th = jnp.tanh(rg * dec_s[0:1, hs])
                mult = lax.rsqrt(0.5 + 0.5 / jnp.maximum(th, 1e-30))
                gx = ig * xcp_s[h, r, :]
                b = mult * gx
                if i >= pad_i:
                    a = jnp.where(rowi == SUBLANES - 1, 1.0, a)
                    b = jnp.where(rowi == SUBLANES - 1, 0.0, b)
                if i == 0 and first is not None:
                    b = jnp.where(jnp.logical_and(rowi == 0, first), gx, b)
                a_s[h, r, :] = a
                b_s[h, r, :] = b
        return run

    def scan_pieces(first):
        st = {}

        def ends():
            hfin = [jnp.zeros((SUBLANES, LANES), jnp.float32) for _ in range(NH)]
            afin = [jnp.ones((SUBLANES, LANES), jnp.float32) for _ in range(NH)]
            for i in range(SEG):
                r = slice(i * SUBLANES, (i + 1) * SUBLANES)
                for h in range(NH):
                    a = a_s[h, r, :]
                    hfin[h] = a * hfin[h] + b_s[h, r, :]
                    afin[h] = afin[h] * a
            st["hfin"], st["afin"] = hfin, afin

        def scan():
            hcur = []
            for h in range(NH):
                cst = hprev_s[h, 0:1, :]
                if first is not None:
                    cst = jnp.where(first, 0.0, cst)
                rows = []
                for s in range(SUBLANES):
                    rows.append(cst)
                    cst = st["afin"][h][s:s + 1, :] * cst + st["hfin"][h][s:s + 1, :]
                hprev_s[h, 0:1, :] = cst
                hcur.append(jnp.concatenate(rows, axis=0))
            for i in range(SEG):
                r = slice(i * SUBLANES, (i + 1) * SUBLANES)
                for h in range(NH):
                    hcur[h] = a_s[h, r, :] * hcur[h] + b_s[h, r, :]
                    h_s[h, pl.ds(i, SUBLANES, stride=SEG), :] = hcur[h]
        return ends, scan

    def lru_out_piece(zb, yr, c, part):
        def run():
            for r in strips(c, part):
                hh = jnp.concatenate([h_s[h, r, :] for h in range(NH)], axis=-1)
                yb = hh * _silu(zb.gb[r, :])
                yr[r, D2:2 * D2] = _rms(yb, gb_ref[...]).astype(yr.dtype)
        return run

    def tile_work(zb, zb_other, hn_cur, yr, first):
        per_grp = D2 // DOT_N
        dots = [dot_piece(hn_cur, zb, grp * D2 + off)
                for grp in DOT_ORDER for off in range(0, D2, DOT_N)]
        first_of = lambda grp: DOT_ORDER.index(grp) * per_grp
        last_of = lambda *grps: max(first_of(grp) for grp in grps) + per_grp - 1
        ends, scan = scan_pieces(first)
        ops = PIECE_OPS
        vpu = ([(gate_piece(zb, zb_other, h, first), ops["gate"], last_of(GRP_XB))
                for h in range(NH)]
               + [(ends, ops["ends"], last_of(GRP_XB)), (scan, ops["scan"], last_of(GRP_XB))]
               + [(gmlp_v_piece(zb, c, part), ops["gmlp_v"], last_of(GRP_V)) for c, part in halves]
               + [(gmlp_mix_piece(range(h, h + NH // 2)), ops["gmlp_mix"], last_of(GRP_V))
                  for h in range(0, NH, NH // 2)]
               + [(gmlp_u_piece(zb, yr, c, part), ops["gmlp_u"], last_of(GRP_V, GRP_U, GRP_GA))
                  for c, part in halves]
               + [(lru_out_piece(zb, yr, c, part), ops["lru_out"], last_of(GRP_XB, GRP_GB))
                  for c, part in halves])
        assert all(a[2] <= b[2] for a, b in zip(vpu, vpu[1:])), "pieces are emitted in list order"
        return dots, vpu

    odd_tile = dict(zb=z[1], other=z[0], hn_cur=hn[1], yr=y_odd_ref, first=None,
                    norm=[norm_piece(0, c, part, hn[0]) for c, part in halves])
    even_tile = dict(zb=z[0], other=z[1], hn_cur=hn[0], yr=y_even_ref,
                     first=(2 * g) % tiles_per_seq == 0,
                     norm=[norm_piece(T, c, part, hn[1]) for c, part in halves])

    def emit(tiles, prologue=()):
        for run in prologue:
            run()
        queue, emitted = [], 0
        for t in tiles:
            dots, vpu = tile_work(t["zb"], t["other"], t["hn_cur"], t["yr"], t["first"])
            queue += [(run, n, emitted + dep + 1) for run, n, dep in vpu]
            fill = list(t["norm"])
            for dot in dots:
                dot()
                emitted += 1
                room = SLOT_OPS
                while queue and queue[0][2] <= emitted and room > 0:
                    run, n, _ = queue.pop(0)
                    run()
                    room -= n
                while fill and room > 0:
                    fill.pop(0)()
                    room -= PIECE_OPS["norm"]
            for run in fill:
                run()
        for run, _, _ in queue:
            run()

    @pl.when(g == 0)
    def _():
        emit([even_tile], prologue=odd_tile["norm"])

    @pl.when(jnp.logical_and(g > 0, g < n_steps - 1))
    def _():
        emit([odd_tile, even_tile])

    @pl.when(g == n_steps - 1)
    def _():
        emit([dict(odd_tile, norm=[])])


def _output_kernel(y_even_ref, y_odd_ref, x_ref, p_ref, w_out_ref, post_g_ref, w_pe_ref,
                   w_pg_ref, o_ref):
    bf16 = jnp.bfloat16
    rows = (slice(0, TILE), slice(TILE, 2 * TILE))
    o = [jnp.dot(y_ref[...], w_out_ref[...], preferred_element_type=jnp.float32)
         for y_ref in (y_even_ref, y_odd_ref)]
    pe = [jnp.dot(p_ref[r, :].astype(bf16), w_pe_ref[...], preferred_element_type=jnp.float32)
          for r in rows]
    h1 = [x_ref[r, :] + _rms(o_h, post_g_ref[...]) for r, o_h in zip(rows, o)]
    gate = [jnp.dot(h.astype(bf16), w_pg_ref[...], preferred_element_type=jnp.float32)
            for h in h1]
    for r, h, pe_h, g_h in zip(rows, h1, pe, gate):
        o_ref[r, :] = h + pe_h * _sigmoid(g_h)


def _const_spec(shape):
    zeros = (0,) * len(shape)
    return pl.BlockSpec(shape, lambda i: zeros, pipeline_mode=pl.Buffered(1))


def _mixer(x, seq_len, pre_g, w_in, ln_g, ln_b, ws, bias_full, cw, cb, wax, b_a, b_x, lam, ga, gb,
           w_out, w_pg):
    N, D = x.shape
    D2 = ln_g.shape[-1]
    NH = D2 // HEAD
    T = TILE
    assert N % (2 * T) == 0 and seq_len % (2 * T) == 0 and NCH * HEAD == 2 * LANES
    G = N // (2 * T)
    f32 = jnp.float32
    consts = [pre_g, w_in, ln_g, ln_b, ws, bias_full, cw, cb, wax, b_a, b_x, lam, ga, gb]
    kern = lambda *refs: _mixer_kernel(*refs, tiles_per_seq=seq_len // T, n_steps=G + 1)
    zbuf = [pltpu.VMEM((T, D2), f32)] * 4 + [pltpu.VMEM((NH, XB_ROWS, LANES), f32)]
    seg_buf = pltpu.VMEM((NH, SEG_ROWS, LANES), f32)
    assert w_out.shape == w_pg.shape and w_out.shape[0] % (G * STRIP) == 0
    slab_spec = pl.BlockSpec((w_out.shape[0] // G, w_out.shape[1]),
                             lambda g: (jnp.minimum(g, G - 1), 0))
    return pl.pallas_call(
        kern,
        grid=(G + 1,),
        in_specs=[pl.BlockSpec((2 * T, D), lambda g: (jnp.minimum(g, G - 1), 0))]
        + [_const_spec(c.shape) for c in consts] + [slab_spec, slab_spec],
        out_specs=[pl.BlockSpec((None, T, 2 * D2), lambda g: (jnp.minimum(g, G - 1), 0, 0)),
                   pl.BlockSpec((None, T, 2 * D2), lambda g: (jnp.maximum(g - 1, 0), 0, 0)),
                   slab_spec, slab_spec],
        out_shape=[jax.ShapeDtypeStruct((G, T, 2 * D2), jnp.bfloat16),
                   jax.ShapeDtypeStruct((G, T, 2 * D2), jnp.bfloat16),
                   jax.ShapeDtypeStruct(w_out.shape, jnp.bfloat16),
                   jax.ShapeDtypeStruct(w_pg.shape, jnp.bfloat16)],
        scratch_shapes=zbuf + zbuf + [
            pltpu.VMEM((T, D), jnp.bfloat16),
            pltpu.VMEM((T, D), jnp.bfloat16),
            pltpu.VMEM((NH, HEAD, HEAD), jnp.bfloat16),
            pltpu.VMEM((2, D2), f32),
            pltpu.VMEM((CONV_WIDTH + 1, NH, SUBLANES, LANES), f32),
            pltpu.VMEM((T, D2), jnp.bfloat16),
            pltpu.VMEM((T, D2), f32),
            seg_buf,
            seg_buf,
            seg_buf,
            seg_buf,
            pltpu.VMEM((NH, SUBLANES, LANES), f32),
        ],
        compiler_params=pltpu.CompilerParams(
            dimension_semantics=("arbitrary",),
            vmem_limit_bytes=VMEM_LIMIT),
        name="mixer",
    )(x, *consts, w_out, w_pg)


def _output(y_even, y_odd, x, p, w_out, post_g, w_pe, w_pg):
    N, D = x.shape
    DP = p.shape[-1]
    T = OUT_TILE
    consts = [w_out, post_g, w_pe, w_pg]
    y_spec = pl.BlockSpec((None, TILE, y_even.shape[-1]), lambda i: (i, 0, 0))
    return pl.pallas_call(
        _output_kernel,
        grid=(N // T,),
        in_specs=[y_spec, y_spec,
                  pl.BlockSpec((T, D), lambda i: (i, 0)),
                  pl.BlockSpec((T, DP), lambda i: (i, 0))]
        + [_const_spec(c.shape) for c in consts],
        out_specs=pl.BlockSpec((T, D), lambda i: (i, 0)),
        out_shape=jax.ShapeDtypeStruct((N, D), jnp.float32),
        compiler_params=pltpu.CompilerParams(
            dimension_semantics=("parallel",),
            vmem_limit_bytes=VMEM_LIMIT),
        name="output",
    )(y_even, y_odd, x, p, *consts)


def kernel(x, p, pre_g, w_in, gmlp_ln_g, gmlp_ln_b, gmlp_ws, gmlp_bs, conv_w, conv_b, w_a, b_a, w_x, b_x, lam, gmlp_out_g, lru_out_g, w_out, post_g, w_pe, w_pg):
    B, S, D = x.shape
    depth = w_in.shape[0]
    D2 = gmlp_ln_g.shape[-1]
    bf16 = jnp.bfloat16
    row = lambda a: a.reshape(1, -1)
    h = x.reshape(B * S, D)
    for l in range(depth):
        bias_full = jnp.repeat(jnp.transpose(gmlp_bs[l]), HEAD, axis=1)
        wax = jnp.concatenate([w_a[l], w_x[l]], axis=-1).astype(bf16)
        y_even, y_odd, w_out_bf16, w_pg_bf16 = _mixer(
            h, S, row(pre_g[l]), w_in[l].astype(bf16), row(gmlp_ln_g[l]), row(gmlp_ln_b[l]),
            gmlp_ws[l], bias_full, conv_w[l].reshape(CONV_WIDTH, D2), row(conv_b[l]),
            wax, row(b_a[l]), row(b_x[l]), row(lam[l]),
            row(gmlp_out_g[l]), row(lru_out_g[l]), w_out[l], w_pg[l])
        h = _output(y_even, y_odd, h, p[l].reshape(B * S, -1),
                    w_out_bf16, row(post_g[l]), w_pe[l].astype(bf16), w_pg_bf16)
    return h.reshape(B, S, D)
```

```python
import collections
import math

import jax
import jax.numpy as jnp
from jax import lax
from jax.experimental import pallas as pl
from jax.experimental.pallas import tpu as pltpu

EPS = 1e-6
LRU_C = 8.0
CONV_WIDTH = 4
LANES = 128
SUBLANES = 8
HEAD = 128
VMEM_LIMIT = 60 * 1024 * 1024

TILE = 256
OUT_TILE = 2 * TILE
SEG = 34
SEG_ROWS = SUBLANES * SEG
XB_ROWS = SUBLANES + SEG_ROWS
NCH = TILE // HEAD
STRIP = 16
DOT_N = 256
GRP_U, GRP_V, GRP_GA, GRP_XB, GRP_GB = range(5)
DOT_ORDER = (GRP_XB, GRP_V, GRP_U, GRP_GA, GRP_GB)
PIECE_OPS = dict(gate=770, ends=816, scan=750, gmlp_v=930, gmlp_mix=50, gmlp_u=1180,
                 lru_out=610, norm=580)
SLOT_OPS = 1500

LOG2E = math.log2(math.e)
GELU_K1 = -2.0 * math.sqrt(2.0 / math.pi) * LOG2E
GELU_K2 = GELU_K1 * 0.044715

ZBuf = collections.namedtuple("ZBuf", "u v ga gb xb")


def _rms(x, g):
    return x * lax.rsqrt(jnp.mean(x * x, axis=-1, keepdims=True) + EPS) * g


def _sigmoid(x):
    return 0.5 * jnp.tanh(0.5 * x) + 0.5


def _silu(x):
    hx = 0.5 * x
    return hx * jnp.tanh(hx) + hx


def _gelu(x):
    return x / (1.0 + jnp.exp2(x * (GELU_K1 + GELU_K2 * (x * x))))


def _mixer_kernel(x_ref, pre_g_ref, w_in_ref, ln_g_ref, ln_b_ref, ws_ref, bias_ref,
                  cw_ref, cb_ref, wa_ref, wx_ref, ba_ref, bx_ref, lam_ref, ga_ref, gb_ref,
                  w_pe_f32_ref, w_out_f32_ref, w_pg_f32_ref,
                  y_even_ref, y_odd_ref, w_pe_bf16_ref, w_out_bf16_ref, w_pg_bf16_ref,
                  *scratch, tiles_per_seq):
    w_out_bf16_ref[...] = w_out_f32_ref[...].astype(jnp.bfloat16)
    w_pg_bf16_ref[...] = w_pg_f32_ref[...].astype(jnp.bfloat16)

    z = (ZBuf(*scratch[0:5]), ZBuf(*scratch[5:10]))
    hn = scratch[10:12]
    wm_s, dec_s, cwb_s, vn_s, mix_s, xcp_s, a_s, b_s, h_s, hprev_s, wax_s = scratch[12:]
    T = TILE
    D2 = z[0].u.shape[-1]
    NH = D2 // HEAD
    g = pl.program_id(0)

    @pl.when(g == 0)
    def _():
        row = lax.broadcasted_iota(jnp.int32, (HEAD, HEAD), 0)
        col = lax.broadcasted_iota(jnp.int32, (HEAD, HEAD), 1)
        for h in range(NH):
            hs = slice(h * HEAD, (h + 1) * HEAD)
            wm_s[h] = jnp.where(row >= col, ws_ref[h], 0.0).astype(jnp.bfloat16)
            wax_s[h, :, 0:HEAD] = wa_ref[h].astype(jnp.bfloat16)
            wax_s[h, :, HEAD:2 * HEAD] = wx_ref[h].astype(jnp.bfloat16)
            for k in range(CONV_WIDTH):
                cwb_s[k, h] = jnp.broadcast_to(cw_ref[k:k + 1, hs], (SUBLANES, LANES))
            cwb_s[CONV_WIDTH, h] = jnp.broadcast_to(cb_ref[:, hs], (SUBLANES, LANES))
        dec = LRU_C * jax.nn.softplus(-lam_ref[...])
        dec_s[0:1, :] = dec
        dec_s[1:2, :] = dec * (-LOG2E)
        w_pe_bf16_ref[...] = w_pe_f32_ref[...].astype(jnp.bfloat16)
        hn[1][...] = jnp.zeros_like(hn[1])
        z[0].xb[...] = jnp.zeros_like(z[0].xb)
        z[1].xb[...] = jnp.zeros_like(z[1].xb)
        hprev_s[...] = jnp.zeros_like(hprev_s)

    def strips(c, part):
        half = HEAD // 2
        r0 = c * HEAD + part * half
        return [slice(s0, s0 + STRIP) for s0 in range(r0, r0 + half, STRIP)]

    halves = [(c, part) for c in range(NCH) for part in range(2)]

    def norm_piece(row0, c, part, hn_s):
        def run():
            for r in strips(c, part):
                hn_s[r, :] = _rms(x_ref[row0 + r.start:row0 + r.stop, :],
                                  pre_g_ref[...]).astype(jnp.bfloat16)
        return run

    def dot_piece(hn_s, zb, col0):
        def run():
            res = jnp.dot(hn_s[...], w_in_ref[:, col0:col0 + DOT_N],
                          preferred_element_type=jnp.float32)
            grp, off = divmod(col0, D2)
            if grp == GRP_XB:
                for hh in range(DOT_N // HEAD):
                    zb.xb[off // HEAD + hh, SUBLANES:SUBLANES + T, :] = (
                        res[:, hh * HEAD:(hh + 1) * HEAD])
            else:
                dst = {GRP_U: zb.u, GRP_V: zb.v, GRP_GA: zb.ga, GRP_GB: zb.gb}[grp]
                dst[:, off:off + DOT_N] = res
        return run

    def gmlp_v_piece(zb, c, part):
        def run():
            for r in strips(c, part):
                v = _gelu(zb.v[r, :])
                mu = jnp.mean(v, axis=-1, keepdims=True)
                vc = v - mu
                vn = vc * lax.rsqrt(jnp.mean(vc * vc, axis=-1, keepdims=True) + EPS)
                vn_s[r, :] = (vn * ln_g_ref[...] + ln_b_ref[...]).astype(jnp.bfloat16)
        return run

    def gmlp_mix_piece(heads):
        def run():
            for h in heads:
                hs = slice(h * HEAD, (h + 1) * HEAD)
                rhs = jnp.concatenate(
                    [vn_s[c * HEAD:(c + 1) * HEAD, hs] for c in range(NCH)], axis=-1)
                m = jnp.dot(wm_s[h], rhs, preferred_element_type=jnp.float32)
                for c in range(NCH):
                    mix_s[c * HEAD:(c + 1) * HEAD, hs] = m[:, c * HEAD:(c + 1) * HEAD]
        return run

    def gmlp_u_piece(zb, yr, c, part):
        def run():
            for r in strips(c, part):
                mixed = mix_s[r, :] + bias_ref[r.start - c * HEAD:r.stop - c * HEAD, :]
                ya = _gelu(zb.u[r, :]) * mixed * _silu(zb.ga[r, :])
                yr[r, 0:D2] = _rms(ya, ga_ref[...]).astype(yr.dtype)
        return run

    def gate_piece(zb, zb_next, h, first):
        def run():
            rowi = lax.broadcasted_iota(jnp.int32, (SUBLANES, LANES), 0)
            hs = slice(h * HEAD, (h + 1) * HEAD)
            if first is not None:
                zb.xb[h, 0:SUBLANES, :] = jnp.where(first, 0.0, zb.xb[h, 0:SUBLANES, :])
            zb_next.xb[h, 0:SUBLANES, :] = zb.xb[h, T:T + SUBLANES, :]
            taps = [zb.xb[h, pl.ds(SUBLANES - (CONV_WIDTH - 1) + j, SUBLANES, stride=SEG), :]
                    for j in range(CONV_WIDTH - 1)]
            for i in range(SEG):
                taps.append(zb.xb[h, pl.ds(SUBLANES + i, SUBLANES, stride=SEG), :])
                xc = cwb_s[CONV_WIDTH, h]
                for k in range(CONV_WIDTH):
                    xc = xc + cwb_s[k, h] * taps[i + k]
                xcp_s[h, i * SUBLANES:(i + 1) * SUBLANES, :] = xc
            gt = jnp.dot(xcp_s[h].astype(jnp.bfloat16), wax_s[h],
                         preferred_element_type=jnp.float32)
            pad_i = T - (SUBLANES - 1) * SEG
            for i in range(SEG):
                r = slice(i * SUBLANES, (i + 1) * SUBLANES)
                rg = _sigmoid(gt[r, :HEAD] + ba_ref[:, hs])
                ig = _sigmoid(gt[r, HEAD:] + bx_ref[:, hs])
                a = jnp.exp2(rg * dec_s[1:2, hs])
                th = jnp.tanh(rg * dec_s[0:1, hs])
                mult = lax.rsqrt(0.5 + 0.5 / jnp.maximum(th, 1e-30))
                gx = ig * xcp_s[h, r, :]
                b = mult * gx
                if i >= pad_i:
                    a = jnp.where(rowi == SUBLANES - 1, 1.0, a)
                    b = jnp.where(rowi == SUBLANES - 1, 0.0, b)
                if i == 0 and first is not None:
                    b = jnp.where(jnp.logical_and(rowi == 0, first), gx, b)
                a_s[h, r, :] = a
                b_s[h, r, :] = b
        return run

    def scan_pieces(first):
        st = {}

        def ends():
            hfin = [jnp.zeros((SUBLANES, LANES), jnp.float32) for _ in range(NH)]
            afin = [jnp.ones((SUBLANES, LANES), jnp.float32) for _ in range(NH)]
            for i in range(SEG):
                r = slice(i * SUBLANES, (i + 1) * SUBLANES)
                for h in range(NH):
                    a = a_s[h, r, :]
                    hfin[h] = a * hfin[h] + b_s[h, r, :]
                    afin[h] = afin[h] * a
            st["hfin"], st["afin"] = hfin, afin

        def scan():
            hcur = []
            for h in range(NH):
                cst = hprev_s[h, 0:1, :]
                if first is not None:
                    cst = jnp.where(first, 0.0, cst)
                rows = []
                for s in range(SUBLANES):
                    rows.append(cst)
                    cst = st["afin"][h][s:s + 1, :] * cst + st["hfin"][h][s:s + 1, :]
                hprev_s[h, 0:1, :] = cst
                hcur.append(jnp.concatenate(rows, axis=0))
            for i in range(SEG):
                r = slice(i * SUBLANES, (i + 1) * SUBLANES)
                for h in range(NH):
                    hcur[h] = a_s[h, r, :] * hcur[h] + b_s[h, r, :]
                    h_s[h, pl.ds(i, SUBLANES, stride=SEG), :] = hcur[h]
        return ends, scan

    def lru_out_piece(zb, yr, c, part):
        def run():
            for r in strips(c, part):
                hh = jnp.concatenate([h_s[h, r, :] for h in range(NH)], axis=-1)
                yb = hh * _silu(zb.gb[r, :])
                yr[r, D2:2 * D2] = _rms(yb, gb_ref[...]).astype(yr.dtype)
        return run

    def tile_work(zb, zb_other, hn_cur, yr, first):
        per_grp = D2 // DOT_N
        dots = [dot_piece(hn_cur, zb, grp * D2 + off)
                for grp in DOT_ORDER for off in range(0, D2, DOT_N)]
        first_of = lambda grp: DOT_ORDER.index(grp) * per_grp
        last_of = lambda *grps: max(first_of(grp) for grp in grps) + per_grp - 1
        ends, scan = scan_pieces(first)
        ops = PIECE_OPS
        vpu = ([(gate_piece(zb, zb_other, h, first), ops["gate"], last_of(GRP_XB))
                for h in range(NH)]
               + [(ends, ops["ends"], last_of(GRP_XB)), (scan, ops["scan"], last_of(GRP_XB))]
               + [(gmlp_v_piece(zb, c, part), ops["gmlp_v"], last_of(GRP_V)) for c, part in halves]
               + [(gmlp_mix_piece(range(h, h + NH // 2)), ops["gmlp_mix"], last_of(GRP_V))
                  for h in range(0, NH, NH // 2)]
               + [(gmlp_u_piece(zb, yr, c, part), ops["gmlp_u"], last_of(GRP_V, GRP_U, GRP_GA))
                  for c, part in halves]
               + [(lru_out_piece(zb, yr, c, part), ops["lru_out"], last_of(GRP_XB, GRP_GB))
                  for c, part in halves])
        assert all(a[2] <= b[2] for a, b in zip(vpu, vpu[1:])), "pieces are emitted in list order"
        return dots, vpu

    tiles = [
        dict(zb=z[1], other=z[0], hn_cur=hn[1], yr=y_odd_ref, first=None,
             norm=[norm_piece(0, c, part, hn[0]) for c, part in halves]),
        dict(zb=z[0], other=z[1], hn_cur=hn[0], yr=y_even_ref,
             first=(2 * g) % tiles_per_seq == 0,
             norm=[norm_piece(T, c, part, hn[1]) for c, part in halves]),
    ]
    queue, emitted = [], 0
    for t in tiles:
        dots, vpu = tile_work(t["zb"], t["other"], t["hn_cur"], t["yr"], t["first"])
        queue += [(run, n, emitted + dep + 1) for run, n, dep in vpu]
        fill = list(t["norm"])
        for dot in dots:
            dot()
            emitted += 1
            room = SLOT_OPS
            while queue and queue[0][2] <= emitted and room > 0:
                run, n, _ = queue.pop(0)
                run()
                room -= n
            while fill and room > 0:
                fill.pop(0)()
                room -= PIECE_OPS["norm"]
        for run in fill:
            run()
    for run, _, _ in queue:
        run()


def _output_kernel(y_even_ref, y_odd_ref, x_ref, p_ref, w_out_ref, post_g_ref, w_pe_ref,
                   w_pg_ref, o_ref):
    bf16 = jnp.bfloat16
    rows = (slice(0, TILE), slice(TILE, 2 * TILE))
    o = [jnp.dot(y_ref[...], w_out_ref[...], preferred_element_type=jnp.float32)
         for y_ref in (y_even_ref, y_odd_ref)]
    pe = [jnp.dot(p_ref[r, :].astype(bf16), w_pe_ref[...], preferred_element_type=jnp.float32)
          for r in rows]
    h1 = [x_ref[r, :] + _rms(o_h, post_g_ref[...]) for r, o_h in zip(rows, o)]
    gate = [jnp.dot(h.astype(bf16), w_pg_ref[...], preferred_element_type=jnp.float32)
            for h in h1]
    for r, h, pe_h, g_h in zip(rows, h1, pe, gate):
        o_ref[r, :] = h + pe_h * _sigmoid(g_h)


def _const_spec(shape):
    zeros = (0,) * len(shape)
    return pl.BlockSpec(shape, lambda i: zeros, pipeline_mode=pl.Buffered(1))


def _mixer(x, seq_len, pre_g, w_in, ln_g, ln_b, ws, bias_full, cw, cb, w_a, w_x, b_a, b_x, lam,
           ga, gb, w_pe, w_out, w_pg):
    N, D = x.shape
    D2 = ln_g.shape[-1]
    NH = D2 // HEAD
    T = TILE
    assert N % (2 * T) == 0 and seq_len % (2 * T) == 0 and NCH * HEAD == 2 * LANES
    G = N // (2 * T)
    f32 = jnp.float32
    consts = [pre_g, w_in, ln_g, ln_b, ws, bias_full, cw, cb, w_a, w_x, b_a, b_x, lam, ga, gb,
              w_pe]
    kern = lambda *refs: _mixer_kernel(*refs, tiles_per_seq=seq_len // T)
    zbuf = [pltpu.VMEM((T, D2), f32)] * 4 + [pltpu.VMEM((NH, XB_ROWS, LANES), f32)]
    seg_buf = pltpu.VMEM((NH, SEG_ROWS, LANES), f32)
    assert w_out.shape == w_pg.shape and w_out.shape[0] % (G * STRIP) == 0
    slab_spec = pl.BlockSpec((w_out.shape[0] // G, w_out.shape[1]),
                             lambda g: (jnp.minimum(g, G - 1), 0))
    return pl.pallas_call(
        kern,
        grid=(G + 1,),
        in_specs=[pl.BlockSpec((2 * T, D), lambda g: (jnp.minimum(g, G - 1), 0))]
        + [_const_spec(c.shape) for c in consts] + [slab_spec, slab_spec],
        out_specs=[pl.BlockSpec((None, T, 2 * D2), lambda g: (g, 0, 0)),
                   pl.BlockSpec((None, T, 2 * D2), lambda g: (jnp.maximum(g - 1, 0), 0, 0)),
                   pl.BlockSpec(w_pe.shape, lambda g: (0, 0)),
                   slab_spec, slab_spec],
        out_shape=[jax.ShapeDtypeStruct((G + 1, T, 2 * D2), jnp.bfloat16),
                   jax.ShapeDtypeStruct((G, T, 2 * D2), jnp.bfloat16),
                   jax.ShapeDtypeStruct(w_pe.shape, jnp.bfloat16),
                   jax.ShapeDtypeStruct(w_out.shape, jnp.bfloat16),
                   jax.ShapeDtypeStruct(w_pg.shape, jnp.bfloat16)],
        scratch_shapes=zbuf + zbuf + [
            pltpu.VMEM((T, D), jnp.bfloat16),
            pltpu.VMEM((T, D), jnp.bfloat16),
            pltpu.VMEM((NH, HEAD, HEAD), jnp.bfloat16),
            pltpu.VMEM((2, D2), f32),
            pltpu.VMEM((CONV_WIDTH + 1, NH, SUBLANES, LANES), f32),
            pltpu.VMEM((T, D2), jnp.bfloat16),
            pltpu.VMEM((T, D2), f32),
            seg_buf,
            seg_buf,
            seg_buf,
            seg_buf,
            pltpu.VMEM((NH, SUBLANES, LANES), f32),
            pltpu.VMEM((NH, HEAD, 2 * HEAD), jnp.bfloat16),
        ],
        compiler_params=pltpu.CompilerParams(
            dimension_semantics=("arbitrary",),
            vmem_limit_bytes=VMEM_LIMIT),
        name="mixer",
    )(x, *consts, w_out, w_pg)


def _output(y_even, y_odd, x, p, w_out, post_g, w_pe, w_pg):
    N, D = x.shape
    DP = p.shape[-1]
    T = OUT_TILE
    consts = [w_out, post_g, w_pe, w_pg]
    y_spec = pl.BlockSpec((None, TILE, y_even.shape[-1]), lambda i: (i, 0, 0))
    return pl.pallas_call(
        _output_kernel,
        grid=(N // T,),
        in_specs=[y_spec, y_spec,
                  pl.BlockSpec((T, D), lambda i: (i, 0)),
                  pl.BlockSpec((T, DP), lambda i: (i, 0))]
        + [_const_spec(c.shape) for c in consts],
        out_specs=pl.BlockSpec((T, D), lambda i: (i, 0)),
        out_shape=jax.ShapeDtypeStruct((N, D), jnp.float32),
        compiler_params=pltpu.CompilerParams(
            dimension_semantics=("parallel",),
            vmem_limit_bytes=VMEM_LIMIT),
        name="output",
    )(y_even, y_odd, x, p, *consts)


def kernel(x, p, pre_g, w_in, gmlp_ln_g, gmlp_ln_b, gmlp_ws, gmlp_bs, conv_w, conv_b, w_a, b_a, w_x, b_x, lam, gmlp_out_g, lru_out_g, w_out, post_g, w_pe, w_pg):
    B, S, D = x.shape
    depth = w_in.shape[0]
    D2 = gmlp_ln_g.shape[-1]
    bf16 = jnp.bfloat16
    row = lambda a: a.reshape(1, -1)
    h = x.reshape(B * S, D)
    for l in range(depth):
        bias_full = jnp.repeat(jnp.transpose(gmlp_bs[l]), HEAD, axis=1)
        y_even, y_odd, w_pe_bf16, w_out_bf16, w_pg_bf16 = _mixer(
            h, S, row(pre_g[l]), w_in[l].astype(bf16), row(gmlp_ln_g[l]), row(gmlp_ln_b[l]),
            gmlp_ws[l], bias_full, conv_w[l].reshape(CONV_WIDTH, D2), row(conv_b[l]),
            w_a[l], w_x[l], row(b_a[l]), row(b_x[l]), row(lam[l]),
            row(gmlp_out_g[l]), row(lru_out_g[l]), w_pe[l], w_out[l], w_pg[l])
        h = _output(y_even, y_odd, h, p[l].reshape(B * S, -1),
                    w_out_bf16, row(post_g[l]), w_pe_bf16, w_pg_bf16)
    return h.reshape(B, S, D)
```
